```python
import math, functools
import jax, jax.numpy as jnp
from jax import lax
import numpy as np

D_MODEL = 2048
BATCH = 2
SEQ = 4096
DEPTH = 2
DEC_BATCH = 8
DEC_SEQ = 8
PAST_LEN = 16384
PAGE_SIZE = 128

D_A = D_MODEL // 2
CONV_A_WIDTH = 31
N_HEADS = D_MODEL // 256
HEAD_DIM = 64
QK_W = N_HEADS * 2 * HEAD_DIM
V_W = N_HEADS * 2 * HEAD_DIM
D_C = D_MODEL // 2
CONV_C_WIDTH = 3
D_FF = 5632
CONV_F_WIDTH = 3
ROPE_THETA = 10000.0
Q_BLOCK = 128
RMS_EPS = 1e-6
LN_EPS = 1e-5
SUBLN_EPS = 1e-5
LAMBDA_STD = 0.1
NEG_INF = -1e30
N_IN = 2 * D_A + 2 * QK_W + V_W + 3 * D_C + 3 * D_MODEL

kernel_name = 'hybrid_conformer_diffattn_shortconv_step'


def _split_points():
    sizes = (D_A, D_A, QK_W, QK_W, V_W, D_C, D_C, D_C, D_MODEL, D_MODEL, D_MODEL)
    return [int(s) for s in np.cumsum(sizes)[:-1]]


def _rmsnorm(x, g, eps):
    xf = x.astype(jnp.float32)
    y = xf * lax.rsqrt(jnp.mean(xf * xf, axis=-1, keepdims=True) + eps)
    return (y * g.astype(jnp.float32)).astype(x.dtype)


def _layernorm(x, g, b, eps):
    xf = x.astype(jnp.float32)
    mu = jnp.mean(xf, axis=-1, keepdims=True)
    var = jnp.mean(jnp.square(xf - mu), axis=-1, keepdims=True)
    y = (xf - mu) * lax.rsqrt(var + eps)
    return (y * g.astype(jnp.float32) + b.astype(jnp.float32)).astype(x.dtype)


def _rope(x, pos):
    half = HEAD_DIM // 2
    inv_freq = 1.0 / (ROPE_THETA ** (jnp.arange(half, dtype=jnp.float32) * 2.0 / HEAD_DIM))
    ang = pos.astype(jnp.float32)[:, None] * inv_freq[None, :]
    cos = jnp.cos(ang)[None, :, None, None, :]
    sin = jnp.sin(ang)[None, :, None, None, :]
    xf = x.astype(jnp.float32)
    x1, x2 = xf[..., :half], xf[..., half:]
    return jnp.concatenate([x1 * cos - x2 * sin, x2 * cos + x1 * sin], axis=-1).astype(x.dtype)


def _causal_dwconv(u, prev, w):
    k_w, c = w.shape
    full = jnp.concatenate([prev.astype(u.dtype), u], axis=1)
    y = lax.conv_general_dilated(full, w[:, None, :].astype(u.dtype), window_strides=(1,),
                                 padding='VALID', dimension_numbers=('NWC', 'WIO', 'NWC'),
                                 feature_group_count=c)
    return y, full[:, full.shape[1] - (k_w - 1):]


def _diff_mix(q, segments, lam):
    scale = HEAD_DIM ** -0.5
    logits = []
    for k, _, mask in segments:
        s = jnp.einsum('bqhmd,bkhmd->bhmqk', q, k, preferred_element_type=jnp.float32) * scale
        if mask is not None:
            s = jnp.where(mask, s, NEG_INF)
        logits.append(s)
    p = jax.nn.softmax(jnp.concatenate(logits, axis=-1), axis=-1)
    w = p[:, :, 0] - lam * p[:, :, 1]
    out = None
    off = 0
    for k, v, _ in segments:
        n = k.shape[1]
        o = jnp.einsum('bhqk,bkhe->bqhe', w[..., off:off + n].astype(v.dtype), v)
        out = o if out is None else out + o
        off += n
    return out


def _attend_prompt(q, k, v, lam):
    b, t = q.shape[0], q.shape[1]
    blk = Q_BLOCK if t % Q_BLOCK == 0 else t
    nb = t // blk
    qb = q.reshape(b, nb, blk, N_HEADS, 2, HEAD_DIM).swapaxes(0, 1)
    starts = jnp.arange(nb, dtype=jnp.int32) * blk
    kpos = jnp.arange(t, dtype=jnp.int32)

    def one_block(args):
        qi, s0 = args
        qpos = s0 + jnp.arange(blk, dtype=jnp.int32)
        mask = kpos[None, :] <= qpos[:, None]
        return _diff_mix(qi, [(k, v, mask)], lam)

    out = lax.map(one_block, (qb, starts))
    return out.swapaxes(0, 1).reshape(b, t, N_HEADS, 2 * HEAD_DIM)


def _attend_sample(q, k, v, lam, k_past, v_past):
    t = q.shape[1]
    mask_new = jnp.tril(jnp.ones((t, t), dtype=bool))
    return _diff_mix(q, [(k_past, v_past, None), (k, v, mask_new)], lam)


def _layer(x, pos, prev_a, prev_c, prev_f, attend, lp, lam_init):
    (g_mix, w_in, wa_conv, ba_conv, ga_ln, ba_ln, wa_out, lq1, lk1, lq2, lk2, g_sub, wb_out,
     wc_conv, wc_out, w_o, g_ffn, w_ffn_in, wf_conv, w_ffn_down) = lp
    b, t, _ = x.shape
    h = _rmsnorm(x, g_mix, RMS_EPS)
    z = h @ w_in
    a_val, a_gate, q, k, v, c_b, c_c, c_x, g_a, g_b, g_c = jnp.split(z, _split_points(), axis=-1)

    u_a = a_val * jax.nn.sigmoid(a_gate)
    conv_a, new_a = _causal_dwconv(u_a, prev_a, wa_conv)
    y_a = jax.nn.silu(_layernorm(conv_a + ba_conv, ga_ln, ba_ln, LN_EPS)) @ wa_out

    q = _rope(q.reshape(b, t, N_HEADS, 2, HEAD_DIM), pos)
    k = _rope(k.reshape(b, t, N_HEADS, 2, HEAD_DIM), pos)
    v = v.reshape(b, t, N_HEADS, 2 * HEAD_DIM)
    lam = (jnp.exp(jnp.sum(lq1.astype(jnp.float32) * lk1.astype(jnp.float32)))
           - jnp.exp(jnp.sum(lq2.astype(jnp.float32) * lk2.astype(jnp.float32))) + lam_init)
    o = attend(q, k, v, lam)
    o = _rmsnorm(o, g_sub, SUBLN_EPS) * (1.0 - lam_init)
    y_b = o.reshape(b, t, V_W) @ wb_out

    conv_c, new_c = _causal_dwconv(c_c * c_x, prev_c, wc_conv)
    y_c = (c_b * conv_c) @ wc_out

    merged = jax.nn.sigmoid(g_a) * y_a + jax.nn.sigmoid(g_b) * y_b + jax.nn.sigmoid(g_c) * y_c
    x = x + merged @ w_o

    h2 = _rmsnorm(x, g_ffn, RMS_EPS)
    f_gate, f_val = jnp.split(h2 @ w_ffn_in, 2, axis=-1)
    f_gate, new_f = _causal_dwconv(f_gate, prev_f, wf_conv)
    x = x + (jax.nn.silu(f_gate) * f_val) @ w_ffn_down
    return x, k.reshape(b, t, N_HEADS, 2 * HEAD_DIM), v, new_a, new_c, new_f


def setup_inputs(seed: int = 0) -> dict:
    key = jax.random.key(seed)
    ks = jax.random.split(key, 32)
    f32 = jnp.float32
    n_pages = PAST_LEN // PAGE_SIZE
    n_used = DEC_BATCH * n_pages
    n_pool = n_used + n_used // 4

    def nrm(k, shape, s):
        return s * jax.random.normal(k, shape, f32)

    x_prompt = nrm(ks[0], (BATCH, SEQ, D_MODEL), 1.0)
    x_sample = nrm(ks[1], (DEC_BATCH, DEC_SEQ, D_MODEL), 1.0)
    cache_k = nrm(ks[2], (DEPTH, n_pool, PAGE_SIZE, N_HEADS, 2 * HEAD_DIM), 1.0)
    cache_v = nrm(ks[3], (DEPTH, n_pool, PAGE_SIZE, N_HEADS, 2 * HEAD_DIM), 1.0)
    state_conv_a = nrm(ks[4], (DEPTH, DEC_BATCH, CONV_A_WIDTH - 1, D_A), 0.5)
    state_conv_c = nrm(ks[5], (DEPTH, DEC_BATCH, CONV_C_WIDTH - 1, D_C), 1.0)
    state_conv_ffn = nrm(ks[6], (DEPTH, DEC_BATCH, CONV_F_WIDTH - 1, D_FF), 1.0)
    page_table = jax.random.permutation(ks[7], n_pool)[:n_used].reshape(DEC_BATCH, n_pages).astype(jnp.int32)

    return {
        'x_prompt': x_prompt,
        'x_sample': x_sample,
        'cache_k': cache_k,
        'cache_v': cache_v,
        'state_conv_a': state_conv_a,
        'state_conv_c': state_conv_c,
        'state_conv_ffn': state_conv_ffn,
        'page_table': page_table,
        'norm_mix': 1.0 + nrm(ks[8], (DEPTH, D_MODEL), 0.01),
        'w_in': nrm(ks[9], (DEPTH, D_MODEL, N_IN), D_MODEL ** -0.5),
        'conv_a_w': nrm(ks[10], (DEPTH, CONV_A_WIDTH, D_A), CONV_A_WIDTH ** -0.5),
        'conv_a_b': nrm(ks[11], (DEPTH, D_A), 0.01),
        'ln_a_g': 1.0 + nrm(ks[12], (DEPTH, D_A), 0.01),
        'ln_a_b': nrm(ks[13], (DEPTH, D_A), 0.01),
        'w_a_out': nrm(ks[14], (DEPTH, D_A, D_MODEL), D_A ** -0.5),
        'lam_q1': nrm(ks[15], (DEPTH, HEAD_DIM), LAMBDA_STD),
        'lam_k1': nrm(ks[16], (DEPTH, HEAD_DIM), LAMBDA_STD),
        'lam_q2': nrm(ks[17], (DEPTH, HEAD_DIM), LAMBDA_STD),
        'lam_k2': nrm(ks[18], (DEPTH, HEAD_DIM), LAMBDA_STD),
        'subln_g': 1.0 + nrm(ks[19], (DEPTH, 2 * HEAD_DIM), 0.01),
        'w_b_out': nrm(ks[20], (DEPTH, V_W, D_MODEL), V_W ** -0.5),
        'conv_c_w': nrm(ks[21], (DEPTH, CONV_C_WIDTH, D_C), CONV_C_WIDTH ** -0.5),
        'w_c_out': nrm(ks[22], (DEPTH, D_C, D_MODEL), D_C ** -0.5),
        'w_o': nrm(ks[23], (DEPTH, D_MODEL, D_MODEL), D_MODEL ** -0.5),
        'norm_ffn': 1.0 + nrm(ks[24], (DEPTH, D_MODEL), 0.01),
        'w_ffn_in': nrm(ks[25], (DEPTH, D_MODEL, 2 * D_FF), D_MODEL ** -0.5),
        'conv_ffn_w': nrm(ks[26], (DEPTH, CONV_F_WIDTH, D_FF), CONV_F_WIDTH ** -0.5),
        'w_ffn_down': nrm(ks[27], (DEPTH, D_FF, D_MODEL), D_FF ** -0.5),
        'norm_final': 1.0 + nrm(ks[28], (D_MODEL,), 0.01),
    }


def reference(x_prompt, x_sample, cache_k, cache_v, state_conv_a, state_conv_c, state_conv_ffn, page_table,
              norm_mix, w_in, conv_a_w, conv_a_b, ln_a_g, ln_a_b, w_a_out, lam_q1, lam_k1, lam_q2, lam_k2,
              subln_g, w_b_out, conv_c_w, w_c_out, w_o, norm_ffn, w_ffn_in, conv_ffn_w, w_ffn_down, norm_final):
    bp, t_p, _ = x_prompt.shape
    bs, t_s, _ = x_sample.shape
    past = page_table.shape[1] * PAGE_SIZE
    pos_p = jnp.arange(t_p, dtype=jnp.int32)
    pos_s = past + jnp.arange(t_s, dtype=jnp.int32)
    dt = x_prompt.dtype
    xp, xs = x_prompt, x_sample
    kp_l, vp_l, ap_l, cp_l, fp_l = [], [], [], [], []
    ks_l, vs_l, as_l, cs_l, fs_l = [], [], [], [], []
    for l in range(DEPTH):
        lp = (norm_mix[l], w_in[l], conv_a_w[l], conv_a_b[l], ln_a_g[l], ln_a_b[l], w_a_out[l],
              lam_q1[l], lam_k1[l], lam_q2[l], lam_k2[l], subln_g[l], w_b_out[l],
              conv_c_w[l], w_c_out[l], w_o[l], norm_ffn[l], w_ffn_in[l], conv_ffn_w[l], w_ffn_down[l])
        lam_init = 0.8 - 0.6 * math.exp(-0.3 * l)

        zero_a = jnp.zeros((bp, CONV_A_WIDTH - 1, D_A), dt)
        zero_c = jnp.zeros((bp, CONV_C_WIDTH - 1, D_C), dt)
        zero_f = jnp.zeros((bp, CONV_F_WIDTH - 1, D_FF), dt)
        xp, kp, vp, ap, cp, fp = _layer(xp, pos_p, zero_a, zero_c, zero_f, _attend_prompt, lp, lam_init)
        kp_l.append(kp); vp_l.append(vp); ap_l.append(ap); cp_l.append(cp); fp_l.append(fp)

        k_past = cache_k[l, page_table].reshape(bs, past, N_HEADS, 2, HEAD_DIM)
        v_past = cache_v[l, page_table].reshape(bs, past, N_HEADS, 2 * HEAD_DIM)
        attend_s = functools.partial(_attend_sample, k_past=k_past, v_past=v_past)
        xs, k_s, v_s, a_s, c_s, f_s = _layer(xs, pos_s, state_conv_a[l], state_conv_c[l], state_conv_ffn[l],
                                             attend_s, lp, lam_init)
        ks_l.append(k_s); vs_l.append(v_s); as_l.append(a_s); cs_l.append(c_s); fs_l.append(f_s)

    y_prompt = _rmsnorm(xp, norm_final, RMS_EPS)
    y_sample = _rmsnorm(xs, norm_final, RMS_EPS)
    return (y_prompt, y_sample,
            jnp.stack(kp_l), jnp.stack(vp_l), jnp.stack(ap_l), jnp.stack(cp_l), jnp.stack(fp_l),
            jnp.stack(ks_l), jnp.stack(vs_l), jnp.stack(as_l), jnp.stack(cs_l), jnp.stack(fs_l))
```

```python
import functools
import math

import jax
import jax.numpy as jnp
from jax import lax
from jax.experimental import pallas as pl
from jax.experimental.pallas import tpu as pltpu

BF16 = jnp.bfloat16
F32 = jnp.float32

HEAD_DIM = 64
PAGE_SIZE = 128
ROPE_THETA = 10000.0
RMS_EPS = 1e-6
LN_EPS = 1e-5
SUBLN_EPS = 1e-5
NEG_INF = -1e30

V7X_LANES = 128
V7X_SUBLANES = 8
V7X_VMEM_LIMIT_BYTES = 60 * 1024 * 1024

HEAD_W = 2 * HEAD_DIM


def _tile(dim, pref):
    if dim <= pref:
        return dim
    t = pref
    while t >= V7X_SUBLANES:
        if dim % t == 0 and t % V7X_SUBLANES == 0:
            return t
        t -= V7X_SUBLANES
    return dim


def _params(sem):
    return pltpu.CompilerParams(dimension_semantics=sem,
                                vmem_limit_bytes=V7X_VMEM_LIMIT_BYTES)


def _dot(a, b):
    return jnp.dot(a, b, preferred_element_type=F32)


def _dot_nt(a, b):
    return lax.dot_general(a, b, (((1,), (1,)), ((), ())), preferred_element_type=F32)


def _rms(x, g, eps):
    return x * lax.rsqrt(jnp.mean(x * x, axis=-1, keepdims=True) + eps) * g


def _rmsnorm_kernel(x_ref, g_ref, o_ref):
    o_ref[...] = _rms(x_ref[...], g_ref[...], RMS_EPS).astype(o_ref.dtype)


def _rmsnorm_bf16(x, g):
    m, d = x.shape
    tm = _tile(m, 512)
    return pl.pallas_call(
        _rmsnorm_kernel,
        grid=(m // tm,),
        in_specs=[pl.BlockSpec((tm, d), lambda i: (i, 0)),
                  pl.BlockSpec((1, d), lambda i: (0, 0))],
        out_specs=pl.BlockSpec((tm, d), lambda i: (i, 0)),
        out_shape=jax.ShapeDtypeStruct((m, d), BF16),
        compiler_params=_params(("parallel",)),
        name="rmsnorm_in",
    )(x, g.reshape(1, d))


def _w_spec(d, tn, layer, col_off):
    assert col_off % tn == 0
    base = col_off // tn
    return pl.BlockSpec((None, d, tn), lambda i, j: (layer, 0, base + j))


def _glu_kernel(h_ref, wv_ref, wg_ref, o_ref):
    h = h_ref[...]
    val = _dot(h, wv_ref[...].astype(BF16))
    gate = _dot(h, wg_ref[...].astype(BF16))
    o_ref[...] = val * jax.nn.sigmoid(gate)


def _proj_glu(h, w_in, layer, off_val, off_gate, width, tm):
    m, d = h.shape
    tn = _tile(width, 512)
    return pl.pallas_call(
        _glu_kernel,
        grid=(m // tm, width // tn),
        in_specs=[pl.BlockSpec((tm, d), lambda i, j: (i, 0)),
                  _w_spec(d, tn, layer, off_val),
                  _w_spec(d, tn, layer, off_gate)],
        out_specs=pl.BlockSpec((tm, tn), lambda i, j: (i, j)),
        out_shape=jax.ShapeDtypeStruct((m, width), F32),
        compiler_params=_params(("parallel", "arbitrary")),
        name="proj_glu",
    )(h, w_in, w_in)


def _rope_tile(x, cos, sin_signed):
    lane = lax.broadcasted_iota(jnp.int32, x.shape, 1)
    first_half = (lane % HEAD_DIM) < (HEAD_DIM // 2)
    partner = jnp.where(first_half,
                        pltpu.roll(x, V7X_LANES - HEAD_DIM // 2, 1),
                        pltpu.roll(x, HEAD_DIM // 2, 1))
    return x * cos + partner * sin_signed


def _heads_kernel(h_ref, w_ref, cos_ref, sin_ref, *o_refs, rope, scale, f32_layout, out_bf16):
    acc = _dot(h_ref[...], w_ref[...].astype(BF16))
    tm, width = acc.shape
    n_heads = width // HEAD_W
    for head in range(n_heads):
        sl = slice(head * HEAD_W, (head + 1) * HEAD_W)
        y = acc[:, sl]
        if rope:
            y = _rope_tile(y, cos_ref[...], sin_ref[...])
        if scale != 1.0:
            y = y * scale
        k = 0
        if f32_layout == "rows":
            o_refs[k][:, sl] = y
            k += 1
        elif f32_layout == "heads":
            o_refs[k][pl.ds(head, tm, stride=n_heads), :] = y
            k += 1
        if out_bf16:
            o_refs[k][:, sl] = y.astype(BF16)


def _proj_heads(h, w_in, layer, off, width, cos, sin, tm, *, rope, scale=1.0, f32_layout=None,
                out_bf16=False):
    m, d = h.shape
    n_heads = width // HEAD_W
    n_pos_blocks = cos.shape[0] // tm
    tab_spec = pl.BlockSpec((tm, V7X_LANES), lambda i: (i % n_pos_blocks, 0))
    out_specs, out_shape = [], []
    if f32_layout == "rows":
        out_specs.append(pl.BlockSpec((tm, width), lambda i: (i, 0)))
        out_shape.append(jax.ShapeDtypeStruct((m, width), F32))
    elif f32_layout == "heads":
        out_specs.append(pl.BlockSpec((tm * n_heads, HEAD_W), lambda i: (i, 0)))
        out_shape.append(jax.ShapeDtypeStruct((m * n_heads, HEAD_W), F32))
    if out_bf16:
        out_specs.append(pl.BlockSpec((tm, width), lambda i: (i, 0)))
        out_shape.append(jax.ShapeDtypeStruct((m, width), BF16))
    assert off % width == 0
    return pl.pallas_call(
        functools.partial(_heads_kernel, rope=rope, scale=scale, f32_layout=f32_layout,
                          out_bf16=out_bf16),
        grid=(m // tm,),
        in_specs=[pl.BlockSpec((tm, d), lambda i: (i, 0)),
                  pl.BlockSpec((None, d, width), lambda i: (layer, 0, off // width)),
                  tab_spec, tab_spec],
        out_specs=out_specs,
        out_shape=out_shape,
        compiler_params=_params(("parallel",)),
        name="proj_heads",
    )(h, w_in, cos, sin)


def _cbranch_kernel(h_ref, wb_ref, wc_ref, wx_ref, cb_ref, cc_ref):
    h = h_ref[...]
    cb_ref[...] = _dot(h, wb_ref[...].astype(BF16)).astype(cb_ref.dtype)
    c_c = _dot(h, wc_ref[...].astype(BF16))
    c_x = _dot(h, wx_ref[...].astype(BF16))
    cc_ref[...] = c_c * c_x


def _proj_cbranch(h, w_in, layer, off_b, off_c, off_x, width, tm):
    m, d = h.shape
    tn = _tile(width, 512)
    return pl.pallas_call(
        _cbranch_kernel,
        grid=(m // tm, width // tn),
        in_specs=[pl.BlockSpec((tm, d), lambda i, j: (i, 0)),
                  _w_spec(d, tn, layer, off_b),
                  _w_spec(d, tn, layer, off_c),
                  _w_spec(d, tn, layer, off_x)],
        out_specs=[pl.BlockSpec((tm, tn), lambda i, j: (i, j)),
                   pl.BlockSpec((tm, tn), lambda i, j: (i, j))],
        out_shape=[jax.ShapeDtypeStruct((m, width), BF16),
                   jax.ShapeDtypeStruct((m, width), F32)],
        compiler_params=_params(("parallel", "arbitrary")),
        name="proj_cbranch",
    )(h, w_in, w_in, w_in)


def _dwconv_fill(u_ref, prev_ref, buf, hist_rows, tt):
    ti = pl.program_id(1)

    @pl.when(ti == 0)
    def _():
        buf[0:hist_rows, :] = prev_ref[...]

    @pl.when(ti > 0)
    def _():
        buf[0:hist_rows, :] = buf[tt:tt + hist_rows, :]

    buf[hist_rows:hist_rows + tt, :] = u_ref[...]


def _dwconv_rows(buf, w_ref, taps, hist_rows, r0, rows, c0, cols):
    base = hist_rows - (taps - 1)
    acc = None
    for k in range(taps):
        term = buf[r0 + base + k:r0 + base + k + rows, c0:c0 + cols] * w_ref[k:k + 1, c0:c0 + cols]
        acc = term if acc is None else acc + term
    return acc


def _conv_ln_silu_kernel(u_ref, prev_ref, w_ref, b_ref, g_ref, beta_ref, o_ref, buf, ybuf,
                         *, taps, hist_rows, tt, row_chunk, col_chunk):
    _dwconv_fill(u_ref, prev_ref, buf, hist_rows, tt)
    c = u_ref.shape[-1]
    for r0 in range(0, tt, row_chunk):
        for c0 in range(0, c, col_chunk):
            ybuf[r0:r0 + row_chunk, c0:c0 + col_chunk] = _dwconv_rows(
                buf, w_ref, taps, hist_rows, r0, row_chunk, c0, col_chunk)
    y = ybuf[...] + b_ref[...]
    mu = jnp.mean(y, axis=-1, keepdims=True)
    yc = y - mu
    var = jnp.mean(yc * yc, axis=-1, keepdims=True)
    z = yc * lax.rsqrt(var + LN_EPS) * g_ref[...] + beta_ref[...]
    o_ref[...] = (z * jax.nn.sigmoid(z)).astype(o_ref.dtype)


def _conv_ln_silu(u, prev, w, bias, ln_g, ln_b, tt):
    b, t, c = u.shape
    taps = w.shape[0]
    hist_rows = prev.shape[1]
    row_chunk = min(tt, 32)
    col_chunk = min(c, 512)
    vec = lambda a: a.reshape(1, c)
    return pl.pallas_call(
        functools.partial(_conv_ln_silu_kernel, taps=taps, hist_rows=hist_rows, tt=tt,
                          row_chunk=row_chunk, col_chunk=col_chunk),
        grid=(b, t // tt),
        in_specs=[pl.BlockSpec((None, tt, c), lambda bi, ti: (bi, ti, 0)),
                  pl.BlockSpec((None, hist_rows, c), lambda bi, ti: (bi, 0, 0)),
                  pl.BlockSpec((taps, c), lambda bi, ti: (0, 0)),
                  pl.BlockSpec((1, c), lambda bi, ti: (0, 0)),
                  pl.BlockSpec((1, c), lambda bi, ti: (0, 0)),
                  pl.BlockSpec((1, c), lambda bi, ti: (0, 0))],
        out_specs=pl.BlockSpec((None, tt, c), lambda bi, ti: (bi, ti, 0)),
        out_shape=jax.ShapeDtypeStruct((b, t, c), BF16),
        scratch_shapes=[pltpu.VMEM((hist_rows + tt, c), F32),
                        pltpu.VMEM((tt, c), F32)],
        compiler_params=_params(("parallel", "arbitrary")),
        name="conv_a",
    )(u, prev, w, vec(bias), vec(ln_g), vec(ln_b))


def _conv_mul_kernel(u_ref, prev_ref, w_ref, cb_ref, o_ref, buf, *, taps, hist_rows, tt):
    _dwconv_fill(u_ref, prev_ref, buf, hist_rows, tt)
    c = u_ref.shape[-1]
    conv = _dwconv_rows(buf, w_ref, taps, hist_rows, 0, tt, 0, c)
    o_ref[...] = (cb_ref[...].astype(F32) * conv).astype(o_ref.dtype)


def _conv_mul(u, prev, w, cb, tt):
    b, t, c = u.shape
    taps = w.shape[0]
    hist_rows = prev.shape[1]
    return pl.pallas_call(
        functools.partial(_conv_mul_kernel, taps=taps, hist_rows=hist_rows, tt=tt),
        grid=(b, t // tt),
        in_specs=[pl.BlockSpec((None, tt, c), lambda bi, ti: (bi, ti, 0)),
                  pl.BlockSpec((None, hist_rows, c), lambda bi, ti: (bi, 0, 0)),
                  pl.BlockSpec((taps, c), lambda bi, ti: (0, 0)),
                  pl.BlockSpec((None, tt, c), lambda bi, ti: (bi, ti, 0))],
        out_specs=pl.BlockSpec((None, tt, c), lambda bi, ti: (bi, ti, 0)),
        out_shape=jax.ShapeDtypeStruct((b, t, c), BF16),
        scratch_shapes=[pltpu.VMEM((hist_rows + tt, c), F32)],
        compiler_params=_params(("parallel", "arbitrary")),
        name="conv_c",
    )(u, prev, w, cb)


def _lambda(lq1_ref, lk1_ref, lq2_ref, lk2_ref, lam_init):
    s1 = jnp.sum(lq1_ref[...] * lk1_ref[...], axis=-1, keepdims=True)
    s2 = jnp.sum(lq2_ref[...] * lk2_ref[...], axis=-1, keepdims=True)
    return jnp.exp(s1) - jnp.exp(s2) + lam_init


def _subln(o, g, lam_init):
    return _rms(o, g, SUBLN_EPS) * (1.0 - lam_init)


def _stack_maps(q_ref, q2_ref, t):
    q = q_ref[...]
    lane = lax.broadcasted_iota(jnp.int32, q.shape, 1)
    zero = jnp.zeros_like(q)
    q2_ref[0:t, :] = jnp.where(lane < HEAD_DIM, q, zero)
    q2_ref[t:2 * t, :] = jnp.where(lane >= HEAD_DIM, q, zero)


def _attn_prompt_kernel(q_ref, k_ref, v_ref, lq1_ref, lk1_ref, lq2_ref, lk2_ref, gs_ref,
                        o_ref, q2_ref, m_ref, l_ref, acc_ref, *, tq, tk, lam_init):
    qi = pl.program_id(2)
    kj = pl.program_id(3)
    last_kj = ((qi + 1) * tq - 1) // tk

    @pl.when(kj == 0)
    def _():
        _stack_maps(q_ref, q2_ref, tq)
        m_ref[...] = jnp.full(m_ref.shape, NEG_INF, F32)
        l_ref[...] = jnp.zeros(l_ref.shape, F32)
        acc_ref[...] = jnp.zeros(acc_ref.shape, F32)

    def step(masked):
        s = _dot_nt(q2_ref[...], k_ref[...])
        if masked:
            row = lax.broadcasted_iota(jnp.int32, s.shape, 0) % tq + qi * tq
            col = lax.broadcasted_iota(jnp.int32, s.shape, 1) + kj * tk
            s = jnp.where(col <= row, s, NEG_INF)
        m_prev = m_ref[...]
        m_new = jnp.maximum(m_prev, jnp.max(s, axis=-1, keepdims=True))
        alpha = jnp.exp(m_prev - m_new)
        p = jnp.exp(s - m_new)
        l_ref[...] = alpha * l_ref[...] + jnp.sum(p, axis=-1, keepdims=True)
        acc_ref[...] = alpha * acc_ref[...] + _dot(p.astype(BF16), v_ref[...])
        m_ref[...] = m_new

    crosses_diag = (kj + 1) * tk - 1 > qi * tq

    @pl.when(jnp.logical_and(kj <= last_kj, crosses_diag))
    def _():
        step(True)

    @pl.when(jnp.logical_and(kj <= last_kj, jnp.logical_not(crosses_diag)))
    def _():
        step(False)

    @pl.when(kj == pl.num_programs(3) - 1)
    def _():
        lam = _lambda(lq1_ref, lk1_ref, lq2_ref, lk2_ref, lam_init)
        o = acc_ref[...] / l_ref[...]
        diff = o[0:tq, :] - lam * o[tq:2 * tq, :]
        o_ref[...] = _subln(diff, gs_ref[...], lam_init).astype(o_ref.dtype)


def _attn_prompt(qb, kb, vb, lam_params, g_sub, lam_init, batch, seq):
    m, w = qb.shape
    n_heads = w // HEAD_W
    tq = _tile(seq, 512)
    tk = _tile(seq, 512)
    nq, nk = seq // tq, seq // tk

    def kv_map(b, h, qi, kj):
        return (b * nk + jnp.minimum(kj, ((qi + 1) * tq - 1) // tk), h)

    small = lambda n: pl.BlockSpec((1, n), lambda b, h, qi, kj: (0, 0))
    return pl.pallas_call(
        functools.partial(_attn_prompt_kernel, tq=tq, tk=tk, lam_init=lam_init),
        grid=(batch, n_heads, nq, nk),
        in_specs=[pl.BlockSpec((tq, HEAD_W), lambda b, h, qi, kj: (b * nq + qi, h)),
                  pl.BlockSpec((tk, HEAD_W), kv_map),
                  pl.BlockSpec((tk, HEAD_W), kv_map),
                  small(HEAD_DIM), small(HEAD_DIM), small(HEAD_DIM), small(HEAD_DIM),
                  small(HEAD_W)],
        out_specs=pl.BlockSpec((tq, HEAD_W), lambda b, h, qi, kj: (b * nq + qi, h)),
        out_shape=jax.ShapeDtypeStruct((m, w), BF16),
        scratch_shapes=[pltpu.VMEM((2 * tq, HEAD_W), BF16),
                        pltpu.VMEM((2 * tq, 1), F32),
                        pltpu.VMEM((2 * tq, 1), F32),
                        pltpu.VMEM((2 * tq, HEAD_W), F32)],
        compiler_params=_params(("parallel", "parallel", "parallel", "arbitrary")),
        name="attn_prompt",
    )(qb, kb, vb, *lam_params, g_sub.reshape(1, HEAD_W))


def _attn_sample_kernel(pt_ref, q_ref, kn_ref, vn_ref, k0_ref, k1_ref, v0_ref, v1_ref,
                        lq1_ref, lk1_ref, lq2_ref, lk2_ref, gs_ref,
                        o_ref, qbd_ref, m_ref, l_ref, acc_ref, *, t_new, n_heads, lam_init):
    p = pl.program_id(1)
    rows_per_head = 2 * t_new

    @pl.when(p == 0)
    def _():
        q = q_ref[...]
        lane = lax.broadcasted_iota(jnp.int32, (t_new, HEAD_W), 1)
        for h in range(n_heads):
            qh = q[:, h * HEAD_W:(h + 1) * HEAD_W]
            r0 = h * rows_per_head
            qbd_ref[r0:r0 + t_new, :] = jnp.where(lane < HEAD_DIM, qh, 0.0)
            qbd_ref[r0 + t_new:r0 + rows_per_head, :] = jnp.where(lane >= HEAD_DIM, qh, 0.0)
        m_ref[...] = jnp.full(m_ref.shape, NEG_INF, F32)
        l_ref[...] = jnp.zeros(l_ref.shape, F32)
        acc_ref[...] = jnp.zeros(acc_ref.shape, F32)

    def update(k, v, causal):
        s = _dot_nt(qbd_ref[...].astype(BF16), k.astype(BF16))
        row = lax.broadcasted_iota(jnp.int32, s.shape, 0)
        col = lax.broadcasted_iota(jnp.int32, s.shape, 1)
        valid = col % n_heads == row // rows_per_head
        if causal:
            valid = jnp.logical_and(valid, col // n_heads <= row % t_new)
        s = jnp.where(valid, s, NEG_INF)
        m_prev = m_ref[...]
        m_new = jnp.maximum(m_prev, jnp.max(s, axis=-1, keepdims=True))
        alpha = jnp.exp(m_prev - m_new)
        pr = jnp.exp(s - m_new)
        l_ref[...] = alpha * l_ref[...] + jnp.sum(pr, axis=-1, keepdims=True)
        acc_ref[...] = alpha * acc_ref[...] + _dot(pr.astype(BF16), v.astype(BF16))
        m_ref[...] = m_new

    page_rows = k0_ref.shape[0] * k0_ref.shape[1]
    for k_ref, v_ref in ((k0_ref, v0_ref), (k1_ref, v1_ref)):
        update(k_ref[...].reshape(page_rows, HEAD_W), v_ref[...].reshape(page_rows, HEAD_W), False)

    @pl.when(p == pl.num_programs(1) - 1)
    def _():
        update(kn_ref[...], vn_ref[...], True)
        lam = _lambda(lq1_ref, lk1_ref, lq2_ref, lk2_ref, lam_init)
        o = acc_ref[...] / l_ref[...]
        for h in range(n_heads):
            r0 = h * rows_per_head
            diff = o[r0:r0 + t_new, :] - lam * o[r0 + t_new:r0 + rows_per_head, :]
            o_ref[:, h * HEAD_W:(h + 1) * HEAD_W] = _subln(
                diff, gs_ref[...], lam_init).astype(o_ref.dtype)


def _attn_sample(q, k_new, v_new, cache_k, cache_v, page_table, layer, lam_params, g_sub,
                 lam_init):
    b, t_new, w = q.shape
    n_heads = w // HEAD_W
    n_pages = page_table.shape[1]
    assert n_pages % 2 == 0 and t_new % V7X_SUBLANES == 0
    rows = 2 * n_heads * t_new

    q_spec = pl.BlockSpec((None, t_new, w), lambda bi, p, pt: (bi, 0, 0))
    new_spec = pl.BlockSpec((None, t_new * n_heads, HEAD_W), lambda bi, p, pt: (bi, 0, 0))
    page_shape = (None, None, PAGE_SIZE, n_heads, HEAD_W)
    page0 = pl.BlockSpec(page_shape, lambda bi, p, pt: (layer, pt[bi, 2 * p], 0, 0, 0))
    page1 = pl.BlockSpec(page_shape, lambda bi, p, pt: (layer, pt[bi, 2 * p + 1], 0, 0, 0))
    small = lambda n: pl.BlockSpec((1, n), lambda bi, p, pt: (0, 0))
    grid_spec = pltpu.PrefetchScalarGridSpec(
        num_scalar_prefetch=1,
        grid=(b, n_pages // 2),
        in_specs=[q_spec, new_spec, new_spec, page0, page1, page0, page1,
                  small(HEAD_DIM), small(HEAD_DIM), small(HEAD_DIM), small(HEAD_DIM),
                  small(HEAD_W)],
        out_specs=pl.BlockSpec((None, t_new, w), lambda bi, p, pt: (bi, 0, 0)),
        scratch_shapes=[pltpu.VMEM((rows, HEAD_W), F32),
                        pltpu.VMEM((rows, 1), F32),
                        pltpu.VMEM((rows, 1), F32),
                        pltpu.VMEM((rows, HEAD_W), F32)],
    )
    return pl.pallas_call(
        functools.partial(_attn_sample_kernel, t_new=t_new, n_heads=n_heads, lam_init=lam_init),
        grid_spec=grid_spec,
        out_shape=jax.ShapeDtypeStruct((b, t_new, w), BF16),
        compiler_params=_params(("parallel", "arbitrary")),
        name="attn_sample",
    )(page_table, q, k_new, v_new, cache_k, cache_k, cache_v, cache_v, *lam_params,
      g_sub.reshape(1, HEAD_W))


def _merge_kernel(ta_ref, tb_ref, tc_ref, h_ref, wa_ref, wb_ref, wc_ref,
                  wga_ref, wgb_ref, wgc_ref, o_ref):
    h = h_ref[...]

    def branch(t_ref, w_ref, wg_ref):
        y = _dot(t_ref[...], w_ref[...].astype(BF16))
        g = _dot(h, wg_ref[...].astype(BF16))
        return jax.nn.sigmoid(g) * y

    merged = (branch(ta_ref, wa_ref, wga_ref) + branch(tb_ref, wb_ref, wgb_ref)
              + branch(tc_ref, wc_ref, wgc_ref))
    o_ref[...] = merged.astype(o_ref.dtype)


def _merge(ta, tb, tc, h, w_a_out, w_b_out, w_c_out, w_in, layer, gate_offs, tm):
    m, d = h.shape
    tn = _tile(d, 256)
    act = lambda a: pl.BlockSpec((tm, a.shape[1]), lambda i, j: (i, 0))
    wout = lambda a: pl.BlockSpec((None, a.shape[1], tn), lambda i, j: (layer, 0, j))
    return pl.pallas_call(
        _merge_kernel,
        grid=(m // tm, d // tn),
        in_specs=[act(ta), act(tb), act(tc), act(h),
                  wout(w_a_out), wout(w_b_out), wout(w_c_out),
                  _w_spec(d, tn, layer, gate_offs[0]),
                  _w_spec(d, tn, layer, gate_offs[1]),
                  _w_spec(d, tn, layer, gate_offs[2])],
        out_specs=pl.BlockSpec((tm, tn), lambda i, j: (i, j)),
        out_shape=jax.ShapeDtypeStruct((m, d), BF16),
        compiler_params=_params(("parallel", "arbitrary")),
        name="merge",
    )(ta, tb, tc, h, w_a_out, w_b_out, w_c_out, w_in, w_in, w_in)


def _accumulate_cols(o_ref, lhs, w_bf16, col_chunk):
    n = o_ref.shape[1]
    for c0 in range(0, n, col_chunk):
        o_ref[:, c0:c0 + col_chunk] += _dot(lhs, w_bf16[:, c0:c0 + col_chunk])


def _wo_kernel(x_ref, mg_ref, w_ref, g_ref, xo_ref, h_ref, *, col_chunk):
    k = pl.program_id(1)

    @pl.when(k == 0)
    def _():
        xo_ref[...] = x_ref[...]

    _accumulate_cols(xo_ref, mg_ref[...], w_ref[...].astype(BF16), col_chunk)

    @pl.when(k == pl.num_programs(1) - 1)
    def _():
        h_ref[...] = _rms(xo_ref[...], g_ref[...], RMS_EPS).astype(h_ref.dtype)


def _wo_residual_norm(x, merged, w_o, layer, g_next, tm):
    m, d = x.shape
    tk = _tile(d, 256)
    return pl.pallas_call(
        functools.partial(_wo_kernel, col_chunk=_tile(d, 512)),
        grid=(m // tm, d // tk),
        in_specs=[pl.BlockSpec((tm, d), lambda i, k: (i, 0)),
                  pl.BlockSpec((tm, tk), lambda i, k: (i, k)),
                  pl.BlockSpec((None, tk, d), lambda i, k: (layer, k, 0)),
                  pl.BlockSpec((1, d), lambda i, k: (0, 0))],
        out_specs=[pl.BlockSpec((tm, d), lambda i, k: (i, 0)),
                   pl.BlockSpec((tm, d), lambda i, k: (i, 0))],
        out_shape=[jax.ShapeDtypeStruct((m, d), F32),
                   jax.ShapeDtypeStruct((m, d), BF16)],
        compiler_params=_params(("parallel", "arbitrary")),
        name="wo_residual_norm",
    )(x, merged, w_o, g_next.reshape(1, d))


def _ffn_kernel(h_ref, x_ref, wg_ref, wv_ref, wc_ref, wd_ref, gn_ref, prev_ref, *refs,
                stream, seq_len, blocks_per_seq, col_chunk, final):
    if final:
        xo_ref, gt_ref, buf_ref, hist_ref = refs
    else:
        xo_ref, hn_ref, gt_ref, buf_ref, hist_ref = refs
    i = pl.program_id(0)
    j = pl.program_id(1)
    h = h_ref[...]
    tm = h.shape[0]
    gate = _dot(h, wg_ref[...].astype(BF16))
    val = _dot(h, wv_ref[...].astype(BF16))
    fc = gate.shape[1]
    hist = V7X_SUBLANES
    nseq = tm // seq_len

    if stream:
        @pl.when(i % blocks_per_seq == 0)
        def _():
            buf_ref[:, 0:hist, :] = jnp.zeros((1, hist, fc), F32)

        @pl.when(i % blocks_per_seq != 0)
        def _():
            buf_ref[:, 0:hist, :] = hist_ref[j]
        hist_ref[j] = gate[tm - hist:tm, :].reshape(1, hist, fc)
        gt_ref[...] = gate[tm - hist:tm, :]
    else:
        buf_ref[:, 0:hist, :] = prev_ref[...]
        gt_ref[...] = gate
    buf_ref[:, hist:hist + seq_len, :] = gate.reshape(nseq, seq_len, fc)

    w = wc_ref[...]
    conv = (buf_ref[:, hist - 2:hist - 2 + seq_len, :] * w[0:1, :].reshape(1, 1, fc)
            + buf_ref[:, hist - 1:hist - 1 + seq_len, :] * w[1:2, :].reshape(1, 1, fc)
            + buf_ref[:, hist:hist + seq_len, :] * w[2:3, :].reshape(1, 1, fc))
    conv = conv.reshape(tm, fc)
    act = (conv * jax.nn.sigmoid(conv) * val).astype(BF16)

    @pl.when(j == 0)
    def _():
        xo_ref[...] = x_ref[...]

    _accumulate_cols(xo_ref, act, wd_ref[...].astype(BF16), col_chunk)

    @pl.when(j == pl.num_programs(1) - 1)
    def _():
        y = _rms(xo_ref[...], gn_ref[...], RMS_EPS)
        if final:
            xo_ref[...] = y
        else:
            hn_ref[...] = y.astype(hn_ref.dtype)


def _ffn(h2, x, w_ffn_in, conv_w, w_ffn_down, layer, g_next, prev, seq_len, tm, final):
    m, d = x.shape
    dff = conv_w.shape[-1]
    fc = _tile(dff, 256)
    nj = dff // fc
    stream = prev is None
    if stream:
        assert seq_len % tm == 0
        blocks_per_seq = seq_len // tm
        rows_in_block = tm
        prev = jnp.zeros((1, V7X_SUBLANES, dff), F32)
        prev_spec = pl.BlockSpec((1, V7X_SUBLANES, fc), lambda i, j: (0, 0, j))
        gt_shape = jax.ShapeDtypeStruct((m // tm, V7X_SUBLANES, dff), F32)
        gt_spec = pl.BlockSpec((None, V7X_SUBLANES, fc), lambda i, j: (i, 0, j))
        nseq = 1
    else:
        assert tm % seq_len == 0 and m == tm
        blocks_per_seq = 1
        rows_in_block = seq_len
        nseq = tm // seq_len
        prev_spec = pl.BlockSpec((nseq, V7X_SUBLANES, fc), lambda i, j: (0, 0, j))
        gt_shape = jax.ShapeDtypeStruct((m, dff), F32)
        gt_spec = pl.BlockSpec((tm, fc), lambda i, j: (i, j))
    row_spec = pl.BlockSpec((tm, d), lambda i, j: (i, 0))
    out_specs = [row_spec] + ([] if final else [row_spec]) + [gt_spec]
    out_shape = ([jax.ShapeDtypeStruct((m, d), F32)]
                 + ([] if final else [jax.ShapeDtypeStruct((m, d), BF16)]) + [gt_shape])
    outs = pl.pallas_call(
        functools.partial(_ffn_kernel, stream=stream, seq_len=rows_in_block,
                          blocks_per_seq=blocks_per_seq, col_chunk=_tile(d, 512), final=final),
        grid=(m // tm, nj),
        in_specs=[pl.BlockSpec((tm, d), lambda i, j: (i, 0)),
                  pl.BlockSpec((tm, d), lambda i, j: (i, 0)),
                  pl.BlockSpec((None, d, fc), lambda i, j: (layer, 0, j)),
                  pl.BlockSpec((None, d, fc), lambda i, j: (layer, 0, nj + j)),
                  pl.BlockSpec((None, conv_w.shape[1], fc), lambda i, j: (layer, 0, j)),
                  pl.BlockSpec((None, fc, d), lambda i, j: (layer, j, 0)),
                  pl.BlockSpec((1, d), lambda i, j: (0, 0)),
                  prev_spec],
        out_specs=out_specs,
        out_shape=out_shape,
        scratch_shapes=[pltpu.VMEM((nseq, V7X_SUBLANES + rows_in_block, fc), F32),
                        pltpu.VMEM((nj, 1, V7X_SUBLANES, fc), F32)],
        compiler_params=_params(("arbitrary", "arbitrary")),
        name="ffn",
    )(h2, x, w_ffn_in, w_ffn_in, conv_w, w_ffn_down, g_next.reshape(1, d), prev)
    if final:
        y, g_part = outs
        return y, None, g_part
    return outs


def _rope_tables(pos):
    half = HEAD_DIM // 2
    inv_freq = 1.0 / (ROPE_THETA ** (jnp.arange(half, dtype=F32) * 2.0 / HEAD_DIM))
    ang = pos.astype(F32)[:, None] * inv_freq[None, :]
    cos, sin = jnp.cos(ang), jnp.sin(ang)
    reps = V7X_LANES // HEAD_DIM
    cos_t = jnp.concatenate([cos, cos] * reps, axis=1)
    sin_t = jnp.concatenate([-sin, sin] * reps, axis=1)
    return cos_t, sin_t


def _pad_history(state, rows):
    b, k1, c = state.shape
    return jnp.concatenate([jnp.zeros((b, rows - k1, c), state.dtype), state], axis=1)


def _layer(x, h, layer, lam_init, final, group, weights):
    (w_in, conv_a_w, conv_a_b, ln_a_g, ln_a_b, w_a_out, lam_q1, lam_k1, lam_q2, lam_k2,
     subln_g, w_b_out, conv_c_w, w_c_out, w_o, norm_ffn, w_ffn_in, conv_ffn_w, w_ffn_down,
     g_next) = weights
    batch, seq, tm = group["batch"], group["seq"], group["tm"]
    m, d = x.shape
    d_a = conv_a_w.shape[-1]
    d_c = conv_c_w.shape[-1]
    d_ff = conv_ffn_w.shape[-1]
    qk_w = w_b_out.shape[1]
    taps_a, taps_c = conv_a_w.shape[1], conv_c_w.shape[1]
    offs = {}
    off = 0
    for name, width in (("a_val", d_a), ("a_gate", d_a), ("q", qk_w), ("k", qk_w), ("v", qk_w),
                        ("c_b", d_c), ("c_c", d_c), ("c_x", d_c),
                        ("g_a", d), ("g_b", d), ("g_c", d)):
        offs[name] = off
        off += width
    assert off == w_in.shape[-1]
    is_sample = group["cache"] is not None
    scale = HEAD_DIM ** -0.5
    lam_params = tuple(a[layer].reshape(1, HEAD_DIM) for a in (lam_q1, lam_k1, lam_q2, lam_k2))

    u_a = _proj_glu(h, w_in, layer, offs["a_val"], offs["a_gate"], d_a, tm)
    u_a3 = u_a.reshape(batch, seq, d_a)
    hist_a = -(-(taps_a - 1) // V7X_SUBLANES) * V7X_SUBLANES
    if is_sample:
        prev_a = _pad_history(group["state_a"][layer], hist_a)
        new_a = jnp.concatenate([group["state_a"][layer], u_a3], axis=1)[:, -(taps_a - 1):]
    else:
        prev_a = jnp.zeros((batch, hist_a, d_a), F32)
        new_a = u_a3[:, seq - (taps_a - 1):]
    t_a = _conv_ln_silu(u_a3, prev_a, conv_a_w[layer], conv_a_b[layer], ln_a_g[layer],
                        ln_a_b[layer], group["tt"]).reshape(m, d_a)

    cos, sin = group["rope"]
    n_heads = qk_w // HEAD_W
    proj = functools.partial(_proj_heads, h, w_in, layer, width=qk_w, cos=cos, sin=sin, tm=tm)
    if is_sample:
        (q_f,) = proj(off=offs["q"], rope=True, scale=scale, f32_layout="rows")
        (k_f,) = proj(off=offs["k"], rope=True, f32_layout="heads")
        (v_f,) = proj(off=offs["v"], rope=False, f32_layout="heads")
        cache_k, cache_v, page_table = group["cache"]
        t_b = _attn_sample(q_f.reshape(batch, seq, qk_w),
                           k_f.reshape(batch, seq * n_heads, HEAD_W),
                           v_f.reshape(batch, seq * n_heads, HEAD_W),
                           cache_k, cache_v, page_table, layer,
                           lam_params, subln_g[layer], lam_init).reshape(m, qk_w)
    else:
        (q_b,) = proj(off=offs["q"], rope=True, scale=scale, out_bf16=True)
        k_f, k_b = proj(off=offs["k"], rope=True, f32_layout="heads", out_bf16=True)
        v_f, v_b = proj(off=offs["v"], rope=False, f32_layout="heads", out_bf16=True)
        t_b = _attn_prompt(q_b, k_b, v_b, lam_params, subln_g[layer], lam_init, batch, seq)

    c_b, cc = _proj_cbranch(h, w_in, layer, offs["c_b"], offs["c_c"], offs["c_x"], d_c, tm)
    cc3 = cc.reshape(batch, seq, d_c)
    if is_sample:
        prev_c = _pad_history(group["state_c"][layer], V7X_SUBLANES)
        new_c = jnp.concatenate([group["state_c"][layer], cc3], axis=1)[:, -(taps_c - 1):]
    else:
        prev_c = jnp.zeros((batch, V7X_SUBLANES, d_c), F32)
        new_c = cc3[:, seq - (taps_c - 1):]
    t_c = _conv_mul(cc3, prev_c, conv_c_w[layer], c_b.reshape(batch, seq, d_c),
                    group["tt"]).reshape(m, d_c)

    merged = _merge(t_a, t_b, t_c, h, w_a_out, w_b_out, w_c_out, w_in, layer,
                    (offs["g_a"], offs["g_b"], offs["g_c"]), tm)
    x_mid, h2 = _wo_residual_norm(x, merged, w_o, layer, norm_ffn[layer], tm)

    taps_f = conv_ffn_w.shape[1]
    if is_sample:
        prev_f = _pad_history(group["state_f"][layer], V7X_SUBLANES)
        x_new, h_next, g_up = _ffn(h2, x_mid, w_ffn_in, conv_ffn_w, w_ffn_down, layer, g_next,
                                   prev_f, seq, tm, final)
        new_f = jnp.concatenate([group["state_f"][layer], g_up.reshape(batch, seq, d_ff)],
                                axis=1)[:, -(taps_f - 1):]
    else:
        x_new, h_next, g_tail = _ffn(h2, x_mid, w_ffn_in, conv_ffn_w, w_ffn_down, layer, g_next,
                                     None, seq, group["tm_ffn"], final)
        blocks_per_seq = seq // group["tm_ffn"]
        g_tail = g_tail.reshape(batch, blocks_per_seq, V7X_SUBLANES, d_ff)
        new_f = g_tail[:, blocks_per_seq - 1, V7X_SUBLANES - (taps_f - 1):]
    k_out = k_f.reshape(batch, seq, n_heads, HEAD_W)
    v_out = v_f.reshape(batch, seq, n_heads, HEAD_W)
    return x_new, h_next, k_out, v_out, new_a, new_c, new_f


def kernel(x_prompt, x_sample, cache_k, cache_v, state_conv_a, state_conv_c, state_conv_ffn,
           page_table, norm_mix, w_in, conv_a_w, conv_a_b, ln_a_g, ln_a_b, w_a_out, lam_q1,
           lam_k1, lam_q2, lam_k2, subln_g, w_b_out, conv_c_w, w_c_out, w_o, norm_ffn, w_ffn_in,
           conv_ffn_w, w_ffn_down, norm_final):
    bp, t_p, d = x_prompt.shape
    bs, t_s, _ = x_sample.shape
    depth = w_in.shape[0]
    past = page_table.shape[1] * PAGE_SIZE

    tm_p = _tile(t_p, 1024)
    cos_p, sin_p = _rope_tables(jnp.arange(t_p, dtype=jnp.int32))
    cos_s, sin_s = _rope_tables(past + jnp.arange(t_s, dtype=jnp.int32))
    groups = [
        dict(batch=bp, seq=t_p, tm=tm_p, tm_ffn=_tile(t_p, 512), tt=_tile(t_p, 256),
             rope=(cos_p, sin_p), cache=None),
        dict(batch=bs, seq=t_s, tm=bs * t_s, tt=t_s,
             rope=(jnp.tile(cos_s, (bs, 1)), jnp.tile(sin_s, (bs, 1))),
             cache=(cache_k, cache_v, page_table),
             state_a=state_conv_a, state_c=state_conv_c, state_f=state_conv_ffn),
    ]
    xs = [x_prompt.reshape(bp * t_p, d), x_sample.reshape(bs * t_s, d)]
    hs = [_rmsnorm_bf16(x, norm_mix[0]) for x in xs]
    collected = [[], []]
    for layer in range(depth):
        lam_init = 0.8 - 0.6 * math.exp(-0.3 * layer)
        final = layer == depth - 1
        g_next = norm_final if final else norm_mix[layer + 1]
        weights = (w_in, conv_a_w, conv_a_b, ln_a_g, ln_a_b, w_a_out, lam_q1, lam_k1, lam_q2,
                   lam_k2, subln_g, w_b_out, conv_c_w, w_c_out, w_o, norm_ffn, w_ffn_in,
                   conv_ffn_w, w_ffn_down, g_next)
        for gi, group in enumerate(groups):
            x_new, h_next, k_out, v_out, new_a, new_c, new_f = _layer(
                xs[gi], hs[gi], layer, lam_init, final, group, weights)
            xs[gi], hs[gi] = x_new, h_next
            collected[gi].append((k_out, v_out, new_a, new_c, new_f))
    y_prompt = xs[0].reshape(bp, t_p, d)
    y_sample = xs[1].reshape(bs, t_s, d)
    stack = lambda gi, idx: jnp.stack([c[idx] for c in collected[gi]])
    return (y_prompt, y_sample,
            stack(0, 0), stack(0, 1), stack(0, 2), stack(0, 3), stack(0, 4),
            stack(1, 0), stack(1, 1), stack(1, 2), stack(1, 3), stack(1, 4))
```

```python
import functools
import math

import jax
import jax.numpy as jnp
from jax import lax
from jax.experimental import pallas as pl
from jax.experimental.pallas import tpu as pltpu

BF16 = jnp.bfloat16
F32 = jnp.float32

HEAD_DIM = 64
PAGE_SIZE = 128
ROPE_THETA = 10000.0
RMS_EPS = 1e-6
LN_EPS = 1e-5
SUBLN_EPS = 1e-5
NEG_INF = -1e30

V7X_LANES = 128
V7X_SUBLANES = 8
V7X_VMEM_LIMIT_BYTES = 60 * 1024 * 1024

HEAD_W = 2 * HEAD_DIM


def _tile(dim, pref):
    if dim <= pref:
        return dim
    t = pref
    while t >= V7X_SUBLANES:
        if dim % t == 0 and t % V7X_SUBLANES == 0:
            return t
        t -= V7X_SUBLANES
    return dim


def _tile_count(total, pref):
    return max(c for c in range(1, min(total, pref) + 1) if total % c == 0)


SAMPLE_PAGES_PER_STEP = 8
PROMPT_ATTN_BLOCK = 512


def _params(sem):
    return pltpu.CompilerParams(dimension_semantics=sem,
                                vmem_limit_bytes=V7X_VMEM_LIMIT_BYTES)


def _dot(a, b):
    return jnp.dot(a, b, preferred_element_type=F32)


def _dot_nt(a, b):
    return lax.dot_general(a, b, (((1,), (1,)), ((), ())), preferred_element_type=F32)


def _rms(x, g, eps):
    return x * lax.rsqrt(jnp.mean(x * x, axis=-1, keepdims=True) + eps) * g


def _rmsnorm_kernel(x_ref, g_ref, o_ref):
    o_ref[...] = _rms(x_ref[...], g_ref[...], RMS_EPS).astype(o_ref.dtype)


def _rmsnorm_bf16(x, g):
    m, d = x.shape
    tm = _tile(m, 512)
    return pl.pallas_call(
        _rmsnorm_kernel,
        grid=(m // tm,),
        in_specs=[pl.BlockSpec((tm, d), lambda i: (i, 0)),
                  pl.BlockSpec((1, d), lambda i: (0, 0))],
        out_specs=pl.BlockSpec((tm, d), lambda i: (i, 0)),
        out_shape=jax.ShapeDtypeStruct((m, d), BF16),
        compiler_params=_params(("parallel",)),
        name="rmsnorm_in",
    )(x, g.reshape(1, d))


def _w_spec(d, tn, layer, col_off):
    assert col_off % tn == 0
    base = col_off // tn
    return pl.BlockSpec((None, d, tn), lambda i, j: (layer, 0, base + j))


def _glu_kernel(h_ref, wv_ref, wg_ref, o_ref):
    h = h_ref[...]
    val = _dot(h, wv_ref[...].astype(BF16))
    gate = _dot(h, wg_ref[...].astype(BF16))
    o_ref[...] = val * jax.nn.sigmoid(gate)


def _proj_glu(h, w_in, layer, off_val, off_gate, width, tm):
    m, d = h.shape
    tn = _tile(width, 512)
    return pl.pallas_call(
        _glu_kernel,
        grid=(m // tm, width // tn),
        in_specs=[pl.BlockSpec((tm, d), lambda i, j: (i, 0)),
                  _w_spec(d, tn, layer, off_val),
                  _w_spec(d, tn, layer, off_gate)],
        out_specs=pl.BlockSpec((tm, tn), lambda i, j: (i, j)),
        out_shape=jax.ShapeDtypeStruct((m, width), F32),
        compiler_params=_params(("parallel", "arbitrary")),
        name="proj_glu",
    )(h, w_in, w_in)


def _rope_tile(x, cos, sin_signed):
    lane = lax.broadcasted_iota(jnp.int32, x.shape, 1)
    first_half = (lane % HEAD_DIM) < (HEAD_DIM // 2)
    partner = jnp.where(first_half,
                        pltpu.roll(x, V7X_LANES - HEAD_DIM // 2, 1),
                        pltpu.roll(x, HEAD_DIM // 2, 1))
    return x * cos + partner * sin_signed


def _heads_kernel(h_ref, w_ref, cos_ref, sin_ref, *o_refs, rope, scale, f32_layout, out_bf16,
                  transposed_tk):
    acc = _dot(h_ref[...], w_ref[...].astype(BF16))
    tm, width = acc.shape
    n_heads = width // HEAD_W
    for head in range(n_heads):
        sl = slice(head * HEAD_W, (head + 1) * HEAD_W)
        y = acc[:, sl]
        if rope:
            y = _rope_tile(y, cos_ref[...], sin_ref[...])
        if scale != 1.0:
            y = y * scale
        k = 0
        if f32_layout == "rows":
            o_refs[k][:, sl] = y
            k += 1
        elif f32_layout == "heads":
            o_refs[k][pl.ds(head, tm, stride=n_heads), :] = y
            k += 1
        if out_bf16:
            o_refs[k][:, sl] = y.astype(BF16)
            k += 1
        if transposed_tk:
            for c in range(tm // transposed_tk):
                rows = slice(c * transposed_tk, (c + 1) * transposed_tk)
                o_refs[k][head, c] = y[rows, :].T.astype(BF16)


def _proj_heads(h, w_in, layer, off, width, cos, sin, tm, *, rope, scale=1.0, f32_layout=None,
                out_bf16=False, transposed_tk=0, seq=None):
    m, d = h.shape
    n_heads = width // HEAD_W
    n_pos_blocks = cos.shape[0] // tm
    tab_spec = pl.BlockSpec((tm, V7X_LANES), lambda i: (i % n_pos_blocks, 0))
    out_specs, out_shape = [], []
    if f32_layout == "rows":
        out_specs.append(pl.BlockSpec((tm, width), lambda i: (i, 0)))
        out_shape.append(jax.ShapeDtypeStruct((m, width), F32))
    elif f32_layout == "heads":
        out_specs.append(pl.BlockSpec((tm * n_heads, HEAD_W), lambda i: (i, 0)))
        out_shape.append(jax.ShapeDtypeStruct((m * n_heads, HEAD_W), F32))
    if out_bf16:
        out_specs.append(pl.BlockSpec((tm, width), lambda i: (i, 0)))
        out_shape.append(jax.ShapeDtypeStruct((m, width), BF16))
    if transposed_tk:
        assert tm % transposed_tk == 0 and seq % tm == 0
        blocks_per_seq = seq // tm
        sub = tm // transposed_tk
        out_specs.append(pl.BlockSpec(
            (None, n_heads, sub, HEAD_W, transposed_tk),
            lambda i: (i // blocks_per_seq, 0, i % blocks_per_seq, 0, 0)))
        out_shape.append(jax.ShapeDtypeStruct(
            (m // seq, n_heads, seq // transposed_tk, HEAD_W, transposed_tk), BF16))
    assert off % width == 0
    return pl.pallas_call(
        functools.partial(_heads_kernel, rope=rope, scale=scale, f32_layout=f32_layout,
                          out_bf16=out_bf16, transposed_tk=transposed_tk),
        grid=(m // tm,),
        in_specs=[pl.BlockSpec((tm, d), lambda i: (i, 0)),
                  pl.BlockSpec((None, d, width), lambda i: (layer, 0, off // width)),
                  tab_spec, tab_spec],
        out_specs=out_specs,
        out_shape=out_shape,
        compiler_params=_params(("parallel",)),
        name="proj_heads",
    )(h, w_in, cos, sin)


def _cbranch_kernel(h_ref, wb_ref, wc_ref, wx_ref, cb_ref, cc_ref):
    h = h_ref[...]
    cb_ref[...] = _dot(h, wb_ref[...].astype(BF16)).astype(cb_ref.dtype)
    c_c = _dot(h, wc_ref[...].astype(BF16))
    c_x = _dot(h, wx_ref[...].astype(BF16))
    cc_ref[...] = c_c * c_x


def _proj_cbranch(h, w_in, layer, off_b, off_c, off_x, width, tm):
    m, d = h.shape
    tn = _tile(width, 512)
    return pl.pallas_call(
        _cbranch_kernel,
        grid=(m // tm, width // tn),
        in_specs=[pl.BlockSpec((tm, d), lambda i, j: (i, 0)),
                  _w_spec(d, tn, layer, off_b),
                  _w_spec(d, tn, layer, off_c),
                  _w_spec(d, tn, layer, off_x)],
        out_specs=[pl.BlockSpec((tm, tn), lambda i, j: (i, j)),
                   pl.BlockSpec((tm, tn), lambda i, j: (i, j))],
        out_shape=[jax.ShapeDtypeStruct((m, width), BF16),
                   jax.ShapeDtypeStruct((m, width), F32)],
        compiler_params=_params(("parallel", "arbitrary")),
        name="proj_cbranch",
    )(h, w_in, w_in, w_in)


def _dwconv_fill(u_ref, prev_ref, buf, hist_rows, tt):
    ti = pl.program_id(1)

    @pl.when(ti == 0)
    def _():
        buf[0:hist_rows, :] = prev_ref[...]

    @pl.when(ti > 0)
    def _():
        buf[0:hist_rows, :] = buf[tt:tt + hist_rows, :]

    buf[hist_rows:hist_rows + tt, :] = u_ref[...]


def _dwconv_rows(buf, w_ref, taps, hist_rows, r0, rows, c0, cols):
    base = hist_rows - (taps - 1)
    acc = None
    for k in range(taps):
        term = buf[r0 + base + k:r0 + base + k + rows, c0:c0 + cols] * w_ref[k:k + 1, c0:c0 + cols]
        acc = term if acc is None else acc + term
    return acc


def _conv_ln_silu_kernel(u_ref, prev_ref, w_ref, b_ref, g_ref, beta_ref, o_ref, buf, ybuf,
                         *, taps, hist_rows, tt, row_chunk, col_chunk):
    _dwconv_fill(u_ref, prev_ref, buf, hist_rows, tt)
    c = u_ref.shape[-1]
    for r0 in range(0, tt, row_chunk):
        for c0 in range(0, c, col_chunk):
            ybuf[r0:r0 + row_chunk, c0:c0 + col_chunk] = _dwconv_rows(
                buf, w_ref, taps, hist_rows, r0, row_chunk, c0, col_chunk)
    y = ybuf[...] + b_ref[...]
    mu = jnp.mean(y, axis=-1, keepdims=True)
    yc = y - mu
    var = jnp.mean(yc * yc, axis=-1, keepdims=True)
    z = yc * lax.rsqrt(var + LN_EPS) * g_ref[...] + beta_ref[...]
    o_ref[...] = (z * jax.nn.sigmoid(z)).astype(o_ref.dtype)


def _conv_ln_silu(u, prev, w, bias, ln_g, ln_b, tt):
    b, t, c = u.shape
    taps = w.shape[0]
    hist_rows = prev.shape[1]
    row_chunk = min(tt, 32)
    col_chunk = min(c, 512)
    vec = lambda a: a.reshape(1, c)
    return pl.pallas_call(
        functools.partial(_conv_ln_silu_kernel, taps=taps, hist_rows=hist_rows, tt=tt,
                          row_chunk=row_chunk, col_chunk=col_chunk),
        grid=(b, t // tt),
        in_specs=[pl.BlockSpec((None, tt, c), lambda bi, ti: (bi, ti, 0)),
                  pl.BlockSpec((None, hist_rows, c), lambda bi, ti: (bi, 0, 0)),
                  pl.BlockSpec((taps, c), lambda bi, ti: (0, 0)),
                  pl.BlockSpec((1, c), lambda bi, ti: (0, 0)),
                  pl.BlockSpec((1, c), lambda bi, ti: (0, 0)),
                  pl.BlockSpec((1, c), lambda bi, ti: (0, 0))],
        out_specs=pl.BlockSpec((None, tt, c), lambda bi, ti: (bi, ti, 0)),
        out_shape=jax.ShapeDtypeStruct((b, t, c), BF16),
        scratch_shapes=[pltpu.VMEM((hist_rows + tt, c), F32),
                        pltpu.VMEM((tt, c), F32)],
        compiler_params=_params(("parallel", "arbitrary")),
        name="conv_a",
    )(u, prev, w, vec(bias), vec(ln_g), vec(ln_b))


def _conv_mul_kernel(u_ref, prev_ref, w_ref, cb_ref, o_ref, buf, *, taps, hist_rows, tt):
    _dwconv_fill(u_ref, prev_ref, buf, hist_rows, tt)
    c = u_ref.shape[-1]
    conv = _dwconv_rows(buf, w_ref, taps, hist_rows, 0, tt, 0, c)
    o_ref[...] = (cb_ref[...].astype(F32) * conv).astype(o_ref.dtype)


def _conv_mul(u, prev, w, cb, tt):
    b, t, c = u.shape
    taps = w.shape[0]
    hist_rows = prev.shape[1]
    return pl.pallas_call(
        functools.partial(_conv_mul_kernel, taps=taps, hist_rows=hist_rows, tt=tt),
        grid=(b, t // tt),
        in_specs=[pl.BlockSpec((None, tt, c), lambda bi, ti: (bi, ti, 0)),
                  pl.BlockSpec((None, hist_rows, c), lambda bi, ti: (bi, 0, 0)),
                  pl.BlockSpec((taps, c), lambda bi, ti: (0, 0)),
                  pl.BlockSpec((None, tt, c), lambda bi, ti: (bi, ti, 0))],
        out_specs=pl.BlockSpec((None, tt, c), lambda bi, ti: (bi, ti, 0)),
        out_shape=jax.ShapeDtypeStruct((b, t, c), BF16),
        scratch_shapes=[pltpu.VMEM((hist_rows + tt, c), F32)],
        compiler_params=_params(("parallel", "arbitrary")),
        name="conv_c",
    )(u, prev, w, cb)


def _lambda(lq1_ref, lk1_ref, lq2_ref, lk2_ref, lam_init):
    s1 = jnp.sum(lq1_ref[...] * lk1_ref[...], axis=-1, keepdims=True)
    s2 = jnp.sum(lq2_ref[...] * lk2_ref[...], axis=-1, keepdims=True)
    return jnp.exp(s1) - jnp.exp(s2) + lam_init


def _subln(o, g, lam_init):
    return _rms(o, g, SUBLN_EPS) * (1.0 - lam_init)


def _attn_prompt_kernel(q_ref, k_ref, vt_ref, lq1_ref, lk1_ref, lq2_ref, lk2_ref, gs_ref,
                        o_ref, m_ref, l_ref, acc_ref, *, tb, lam_init):
    qi = pl.program_id(2)
    q = q_ref[...]
    lane = lax.broadcasted_iota(jnp.int32, q.shape, 1)
    zero = jnp.zeros_like(q)
    q_maps = (jnp.where(lane < HEAD_DIM, q, zero), jnp.where(lane >= HEAD_DIM, q, zero))
    m_ref[...] = jnp.full(m_ref.shape, NEG_INF, F32)
    l_ref[...] = jnp.zeros(l_ref.shape, F32)
    acc_ref[...] = jnp.zeros(acc_ref.shape, F32)

    def block(kj, masked):
        k = k_ref[pl.ds(pl.multiple_of(kj * tb, tb), tb), :]
        vt = vt_ref[kj]
        for mp in range(2):
            s = _dot_nt(k, q_maps[mp])
            if masked:
                key = lax.broadcasted_iota(jnp.int32, s.shape, 0)
                qry = lax.broadcasted_iota(jnp.int32, s.shape, 1)
                s = jnp.where(key <= qry, s, NEG_INF)
            m_prev = m_ref[mp]
            m_new = jnp.maximum(m_prev, jnp.max(s, axis=0, keepdims=True))
            alpha = jnp.exp(m_prev - m_new)
            p = jnp.exp(s - m_new)
            l_ref[mp] = alpha * l_ref[mp] + jnp.sum(p, axis=0, keepdims=True)
            acc_ref[mp] = alpha * acc_ref[mp] + _dot(vt, p.astype(BF16))
            m_ref[mp] = m_new

    def full_block(kj, carry):
        block(kj, False)
        return carry

    lax.fori_loop(0, qi, full_block, 0)
    block(qi, True)

    lam = _lambda(lq1_ref, lk1_ref, lq2_ref, lk2_ref, lam_init)
    diff = acc_ref[0] / l_ref[0] - lam * (acc_ref[1] / l_ref[1])
    ms = jnp.mean(diff * diff, axis=0, keepdims=True)
    normed = diff * lax.rsqrt(ms + SUBLN_EPS) * gs_ref[...] * (1.0 - lam_init)
    o_ref[...] = normed.T.astype(o_ref.dtype)


def _attn_prompt(qb, kb, vt, lam_params, g_sub, lam_init, batch, seq, tb):
    m, w = qb.shape
    n_heads = w // HEAD_W
    nq = seq // tb
    small = lambda n: pl.BlockSpec((1, n), lambda b, h, qi: (0, 0))
    return pl.pallas_call(
        functools.partial(_attn_prompt_kernel, tb=tb, lam_init=lam_init),
        grid=(batch, n_heads, nq),
        in_specs=[pl.BlockSpec((tb, HEAD_W), lambda b, h, qi: (b * nq + qi, h)),
                  pl.BlockSpec((seq, HEAD_W), lambda b, h, qi: (b, h)),
                  pl.BlockSpec((None, None, nq, HEAD_W, tb), lambda b, h, qi: (b, h, 0, 0, 0)),
                  small(HEAD_DIM), small(HEAD_DIM), small(HEAD_DIM), small(HEAD_DIM),
                  pl.BlockSpec((HEAD_W, 1), lambda b, h, qi: (0, 0))],
        out_specs=pl.BlockSpec((tb, HEAD_W), lambda b, h, qi: (b * nq + qi, h)),
        out_shape=jax.ShapeDtypeStruct((m, w), BF16),
        scratch_shapes=[pltpu.VMEM((2, 1, tb), F32),
                        pltpu.VMEM((2, 1, tb), F32),
                        pltpu.VMEM((2, HEAD_W, tb), F32)],
        compiler_params=_params(("parallel", "parallel", "arbitrary")),
        name="attn_prompt",
    )(qb, kb, vt, *lam_params, g_sub.reshape(HEAD_W, 1))


def _attn_sample_kernel(pt_ref, q_ref, kn_ref, vn_ref, *refs, t_new, n_heads, lam_init, n_pages):
    page_refs = refs[:2 * n_pages]
    (lq1_ref, lk1_ref, lq2_ref, lk2_ref, gs_ref, o_ref,
     qbd_ref, bias_ref, m_ref, l_ref, acc_ref) = refs[2 * n_pages:]
    del pt_ref
    p = pl.program_id(1)
    rows_per_head = 2 * t_new

    @pl.when(p == 0)
    def _():
        q = q_ref[...]
        lane = lax.broadcasted_iota(jnp.int32, (t_new, HEAD_W), 1)
        for h in range(n_heads):
            qh = q[:, h * HEAD_W:(h + 1) * HEAD_W]
            r0 = h * rows_per_head
            qbd_ref[r0:r0 + t_new, :] = jnp.where(lane < HEAD_DIM, qh, 0.0)
            qbd_ref[r0 + t_new:r0 + rows_per_head, :] = jnp.where(lane >= HEAD_DIM, qh, 0.0)
        m_ref[...] = jnp.full(m_ref.shape, NEG_INF, F32)
        l_ref[...] = jnp.zeros(l_ref.shape, F32)
        acc_ref[...] = jnp.zeros(acc_ref.shape, F32)

        row = lax.broadcasted_iota(jnp.int32, bias_ref.shape, 0)
        col = lax.broadcasted_iota(jnp.int32, bias_ref.shape, 1)
        bias_ref[...] = jnp.where(col % n_heads == row // rows_per_head, 0.0, NEG_INF)

    def update(scores, values):
        m_prev = m_ref[...]
        m_new = m_prev
        for s in scores:
            m_new = jnp.maximum(m_new, jnp.max(s, axis=-1, keepdims=True))
        alpha = jnp.exp(m_prev - m_new)
        l_new = alpha * l_ref[...]
        acc = alpha * acc_ref[...]
        for s, v in zip(scores, values):
            pr = jnp.exp(s - m_new)
            l_new = l_new + jnp.sum(pr, axis=-1, keepdims=True)
            acc = acc + _dot(pr.astype(BF16), v.astype(BF16))
        l_ref[...] = l_new
        acc_ref[...] = acc
        m_ref[...] = m_new

    qbd = qbd_ref[...].astype(BF16)
    page_rows = PAGE_SIZE * n_heads
    bias = bias_ref[...]
    scores = [_dot_nt(qbd, k_ref[...].reshape(page_rows, HEAD_W).astype(BF16)) + bias
              for k_ref in page_refs[:n_pages]]
    values = [v_ref[...].reshape(page_rows, HEAD_W) for v_ref in page_refs[n_pages:]]
    update(scores, values)

    @pl.when(p == pl.num_programs(1) - 1)
    def _():
        s = _dot_nt(qbd, kn_ref[...].astype(BF16))
        row = lax.broadcasted_iota(jnp.int32, s.shape, 0)
        col = lax.broadcasted_iota(jnp.int32, s.shape, 1)
        valid = jnp.logical_and(col % n_heads == row // rows_per_head,
                                col // n_heads <= row % t_new)
        update([jnp.where(valid, s, NEG_INF)], [vn_ref[...]])
        lam = _lambda(lq1_ref, lk1_ref, lq2_ref, lk2_ref, lam_init)
        o = acc_ref[...] / l_ref[...]
        for h in range(n_heads):
            r0 = h * rows_per_head
            diff = o[r0:r0 + t_new, :] - lam * o[r0 + t_new:r0 + rows_per_head, :]
            o_ref[:, h * HEAD_W:(h + 1) * HEAD_W] = _subln(
                diff, gs_ref[...], lam_init).astype(o_ref.dtype)


def _attn_sample(q, k_new, v_new, cache_k, cache_v, page_table, layer, lam_params, g_sub,
                 lam_init):
    b, t_new, w = q.shape
    n_heads = w // HEAD_W
    total_pages = page_table.shape[1]
    n_pages = _tile_count(total_pages, SAMPLE_PAGES_PER_STEP)
    assert t_new % V7X_SUBLANES == 0
    rows = 2 * n_heads * t_new

    q_spec = pl.BlockSpec((None, t_new, w), lambda bi, p, pt: (bi, 0, 0))
    new_spec = pl.BlockSpec((None, t_new * n_heads, HEAD_W), lambda bi, p, pt: (bi, 0, 0))
    page_shape = (None, None, PAGE_SIZE, n_heads, HEAD_W)
    page_specs = [
        pl.BlockSpec(page_shape, lambda bi, p, pt, c=c: (layer, pt[bi, n_pages * p + c], 0, 0, 0))
        for c in range(n_pages)]
    small = lambda n: pl.BlockSpec((1, n), lambda bi, p, pt: (0, 0))
    grid_spec = pltpu.PrefetchScalarGridSpec(
        num_scalar_prefetch=1,
        grid=(b, total_pages // n_pages),
        in_specs=[q_spec, new_spec, new_spec] + page_specs + page_specs
                 + [small(HEAD_DIM), small(HEAD_DIM), small(HEAD_DIM), small(HEAD_DIM),
                    small(HEAD_W)],
        out_specs=pl.BlockSpec((None, t_new, w), lambda bi, p, pt: (bi, 0, 0)),
        scratch_shapes=[pltpu.VMEM((rows, HEAD_W), F32),
                        pltpu.VMEM((rows, PAGE_SIZE * n_heads), F32),
                        pltpu.VMEM((rows, 1), F32),
                        pltpu.VMEM((rows, 1), F32),
                        pltpu.VMEM((rows, HEAD_W), F32)],
    )
    return pl.pallas_call(
        functools.partial(_attn_sample_kernel, t_new=t_new, n_heads=n_heads, lam_init=lam_init,
                          n_pages=n_pages),
        grid_spec=grid_spec,
        out_shape=jax.ShapeDtypeStruct((b, t_new, w), BF16),
        compiler_params=_params(("parallel", "arbitrary")),
        name="attn_sample",
    )(page_table, q, k_new, v_new, *([cache_k] * n_pages), *([cache_v] * n_pages), *lam_params,
      g_sub.reshape(1, HEAD_W))


def _merge_kernel(ta_ref, tb_ref, tc_ref, h_ref, wa_ref, wb_ref, wc_ref,
                  wga_ref, wgb_ref, wgc_ref, o_ref):
    h = h_ref[...]

    def branch(t_ref, w_ref, wg_ref):
        y = _dot(t_ref[...], w_ref[...].astype(BF16))
        g = _dot(h, wg_ref[...].astype(BF16))
        return jax.nn.sigmoid(g) * y

    merged = (branch(ta_ref, wa_ref, wga_ref) + branch(tb_ref, wb_ref, wgb_ref)
              + branch(tc_ref, wc_ref, wgc_ref))
    o_ref[...] = merged.astype(o_ref.dtype)


def _merge(ta, tb, tc, h, w_a_out, w_b_out, w_c_out, w_in, layer, gate_offs, tm):
    m, d = h.shape
    tn = _tile(d, 256)
    act = lambda a: pl.BlockSpec((tm, a.shape[1]), lambda i, j: (i, 0))
    wout = lambda a: pl.BlockSpec((None, a.shape[1], tn), lambda i, j: (layer, 0, j))
    return pl.pallas_call(
        _merge_kernel,
        grid=(m // tm, d // tn),
        in_specs=[act(ta), act(tb), act(tc), act(h),
                  wout(w_a_out), wout(w_b_out), wout(w_c_out),
                  _w_spec(d, tn, layer, gate_offs[0]),
                  _w_spec(d, tn, layer, gate_offs[1]),
                  _w_spec(d, tn, layer, gate_offs[2])],
        out_specs=pl.BlockSpec((tm, tn), lambda i, j: (i, j)),
        out_shape=jax.ShapeDtypeStruct((m, d), BF16),
        compiler_params=_params(("parallel", "arbitrary")),
        name="merge",
    )(ta, tb, tc, h, w_a_out, w_b_out, w_c_out, w_in, w_in, w_in)


def _accumulate_cols(o_ref, lhs, w_bf16, col_chunk):
    n = o_ref.shape[1]
    for c0 in range(0, n, col_chunk):
        o_ref[:, c0:c0 + col_chunk] += _dot(lhs, w_bf16[:, c0:c0 + col_chunk])


def _wo_kernel(x_ref, mg_ref, w_ref, g_ref, xo_ref, h_ref, *, col_chunk):
    k = pl.program_id(1)

    @pl.when(k == 0)
    def _():
        xo_ref[...] = x_ref[...]

    _accumulate_cols(xo_ref, mg_ref[...], w_ref[...], col_chunk)

    @pl.when(k == pl.num_programs(1) - 1)
    def _():
        h_ref[...] = _rms(xo_ref[...], g_ref[...], RMS_EPS).astype(h_ref.dtype)


def _wo_residual_norm(x, merged, w_o, layer, g_next, tm):
    m, d = x.shape
    tk = _tile(d, 1024)
    return pl.pallas_call(
        functools.partial(_wo_kernel, col_chunk=_tile(d, 512)),
        grid=(m // tm, d // tk),
        in_specs=[pl.BlockSpec((tm, d), lambda i, k: (i, 0)),
                  pl.BlockSpec((tm, tk), lambda i, k: (i, k)),
                  pl.BlockSpec((None, tk, d), lambda i, k: (layer, k, 0)),
                  pl.BlockSpec((1, d), lambda i, k: (0, 0))],
        out_specs=[pl.BlockSpec((tm, d), lambda i, k: (i, 0)),
                   pl.BlockSpec((tm, d), lambda i, k: (i, 0))],
        out_shape=[jax.ShapeDtypeStruct((m, d), F32),
                   jax.ShapeDtypeStruct((m, d), BF16)],
        compiler_params=_params(("parallel", "arbitrary")),
        name="wo_residual_norm",
    )(x, merged, w_o, g_next.reshape(1, d))


def _ffn_kernel(h_ref, x_ref, wg_ref, wv_ref, wc_ref, wd_ref, gn_ref, prev_ref, *refs,
                stream, seq_len, blocks_per_seq, col_chunk, final):
    if final:
        xo_ref, gt_ref, buf_ref, hist_ref = refs
    else:
        xo_ref, hn_ref, gt_ref, buf_ref, hist_ref = refs
    i = pl.program_id(0)
    j = pl.program_id(1)
    h = h_ref[...]
    tm = h.shape[0]
    gate = _dot(h, wg_ref[...])
    val = _dot(h, wv_ref[...])
    fc = gate.shape[1]
    hist = V7X_SUBLANES
    nseq = tm // seq_len

    if stream:
        @pl.when(i % blocks_per_seq == 0)
        def _():
            buf_ref[:, 0:hist, :] = jnp.zeros((1, hist, fc), F32)

        @pl.when(i % blocks_per_seq != 0)
        def _():
            buf_ref[:, 0:hist, :] = hist_ref[j]
        hist_ref[j] = gate[tm - hist:tm, :].reshape(1, hist, fc)
        gt_ref[...] = gate[tm - hist:tm, :]
    else:
        buf_ref[:, 0:hist, :] = prev_ref[...]
        gt_ref[...] = gate
    buf_ref[:, hist:hist + seq_len, :] = gate.reshape(nseq, seq_len, fc)

    w = wc_ref[...]
    conv = (buf_ref[:, hist - 2:hist - 2 + seq_len, :] * w[0:1, :].reshape(1, 1, fc)
            + buf_ref[:, hist - 1:hist - 1 + seq_len, :] * w[1:2, :].reshape(1, 1, fc)
            + buf_ref[:, hist:hist + seq_len, :] * w[2:3, :].reshape(1, 1, fc))
    conv = conv.reshape(tm, fc)
    act = (conv * jax.nn.sigmoid(conv) * val).astype(BF16)

    @pl.when(j == 0)
    def _():
        xo_ref[...] = x_ref[...]

    _accumulate_cols(xo_ref, act, wd_ref[...], col_chunk)

    @pl.when(j == pl.num_programs(1) - 1)
    def _():
        y = _rms(xo_ref[...], gn_ref[...], RMS_EPS)
        if final:
            xo_ref[...] = y
        else:
            hn_ref[...] = y.astype(hn_ref.dtype)


def _ffn(h2, x, w_ffn_in, conv_w, w_ffn_down, layer, g_next, prev, seq_len, tm, final):
    m, d = x.shape
    dff = conv_w.shape[-1]
    fc = _tile(dff, 512)
    nj = dff // fc
    stream = prev is None
    if stream:
        assert seq_len % tm == 0
        blocks_per_seq = seq_len // tm
        rows_in_block = tm
        prev = jnp.zeros((1, V7X_SUBLANES, dff), F32)
        prev_spec = pl.BlockSpec((1, V7X_SUBLANES, fc), lambda i, j: (0, 0, j))
        gt_shape = jax.ShapeDtypeStruct((m // tm, V7X_SUBLANES, dff), F32)
        gt_spec = pl.BlockSpec((None, V7X_SUBLANES, fc), lambda i, j: (i, 0, j))
        nseq = 1
    else:
        assert tm % seq_len == 0 and m == tm
        blocks_per_seq = 1
        rows_in_block = seq_len
        nseq = tm // seq_len
        prev_spec = pl.BlockSpec((nseq, V7X_SUBLANES, fc), lambda i, j: (0, 0, j))
        gt_shape = jax.ShapeDtypeStruct((m, dff), F32)
        gt_spec = pl.BlockSpec((tm, fc), lambda i, j: (i, j))
    row_spec = pl.BlockSpec((tm, d), lambda i, j: (i, 0))
    out_specs = [row_spec] + ([] if final else [row_spec]) + [gt_spec]
    out_shape = ([jax.ShapeDtypeStruct((m, d), F32)]
                 + ([] if final else [jax.ShapeDtypeStruct((m, d), BF16)]) + [gt_shape])
    outs = pl.pallas_call(
        functools.partial(_ffn_kernel, stream=stream, seq_len=rows_in_block,
                          blocks_per_seq=blocks_per_seq, col_chunk=_tile(d, 512), final=final),
        grid=(m // tm, nj),
        in_specs=[pl.BlockSpec((tm, d), lambda i, j: (i, 0)),
                  pl.BlockSpec((tm, d), lambda i, j: (i, 0)),
                  pl.BlockSpec((None, d, fc), lambda i, j: (layer, 0, j)),
                  pl.BlockSpec((None, d, fc), lambda i, j: (layer, 0, nj + j)),
                  pl.BlockSpec((None, conv_w.shape[1], fc), lambda i, j: (layer, 0, j)),
                  pl.BlockSpec((None, fc, d), lambda i, j: (layer, j, 0)),
                  pl.BlockSpec((1, d), lambda i, j: (0, 0)),
                  prev_spec],
        out_specs=out_specs,
        out_shape=out_shape,
        scratch_shapes=[pltpu.VMEM((nseq, V7X_SUBLANES + rows_in_block, fc), F32),
                        pltpu.VMEM((nj, 1, V7X_SUBLANES, fc), F32)],
        compiler_params=_params(("arbitrary", "arbitrary")),
        name="ffn",
    )(h2, x, w_ffn_in, w_ffn_in, conv_w, w_ffn_down, g_next.reshape(1, d), prev)
    if final:
        y, g_part = outs
        return y, None, g_part
    return outs


def _rope_tables(pos):
    half = HEAD_DIM // 2
    inv_freq = 1.0 / (ROPE_THETA ** (jnp.arange(half, dtype=F32) * 2.0 / HEAD_DIM))
    ang = pos.astype(F32)[:, None] * inv_freq[None, :]
    cos, sin = jnp.cos(ang), jnp.sin(ang)
    reps = V7X_LANES // HEAD_DIM
    cos_t = jnp.concatenate([cos, cos] * reps, axis=1)
    sin_t = jnp.concatenate([-sin, sin] * reps, axis=1)
    return cos_t, sin_t


def _pad_history(state, rows):
    b, k1, c = state.shape
    return jnp.concatenate([jnp.zeros((b, rows - k1, c), state.dtype), state], axis=1)


def _layer(x, h, layer, lam_init, final, group, weights):
    (w_in, conv_a_w, conv_a_b, ln_a_g, ln_a_b, w_a_out, lam_q1, lam_k1, lam_q2, lam_k2,
     subln_g, w_b_out, conv_c_w, w_c_out, w_o, norm_ffn, w_ffn_in, conv_ffn_w, w_ffn_down,
     g_next) = weights
    batch, seq, tm = group["batch"], group["seq"], group["tm"]
    m, d = x.shape
    d_a = conv_a_w.shape[-1]
    d_c = conv_c_w.shape[-1]
    d_ff = conv_ffn_w.shape[-1]
    qk_w = w_b_out.shape[1]
    taps_a, taps_c = conv_a_w.shape[1], conv_c_w.shape[1]
    offs = {}
    off = 0
    for name, width in (("a_val", d_a), ("a_gate", d_a), ("q", qk_w), ("k", qk_w), ("v", qk_w),
                        ("c_b", d_c), ("c_c", d_c), ("c_x", d_c),
                        ("g_a", d), ("g_b", d), ("g_c", d)):
        offs[name] = off
        off += width
    assert off == w_in.shape[-1]
    is_sample = group["cache"] is not None
    scale = HEAD_DIM ** -0.5
    lam_params = tuple(a[layer].reshape(1, HEAD_DIM) for a in (lam_q1, lam_k1, lam_q2, lam_k2))

    u_a = _proj_glu(h, w_in, layer, offs["a_val"], offs["a_gate"], d_a, tm)
    u_a3 = u_a.reshape(batch, seq, d_a)
    hist_a = -(-(taps_a - 1) // V7X_SUBLANES) * V7X_SUBLANES
    if is_sample:
        prev_a = _pad_history(group["state_a"][layer], hist_a)
        new_a = jnp.concatenate([group["state_a"][layer], u_a3], axis=1)[:, -(taps_a - 1):]
    else:
        prev_a = jnp.zeros((batch, hist_a, d_a), F32)
        new_a = u_a3[:, seq - (taps_a - 1):]
    t_a = _conv_ln_silu(u_a3, prev_a, conv_a_w[layer], conv_a_b[layer], ln_a_g[layer],
                        ln_a_b[layer], group["tt"]).reshape(m, d_a)

    cos, sin = group["rope"]
    n_heads = qk_w // HEAD_W
    proj = functools.partial(_proj_heads, h, w_in, layer, width=qk_w, cos=cos, sin=sin, tm=tm)
    if is_sample:
        (q_f,) = proj(off=offs["q"], rope=True, scale=scale, f32_layout="rows")
        (k_f,) = proj(off=offs["k"], rope=True, f32_layout="heads")
        (v_f,) = proj(off=offs["v"], rope=False, f32_layout="heads")
        cache_k, cache_v, page_table = group["cache"]
        t_b = _attn_sample(q_f.reshape(batch, seq, qk_w),
                           k_f.reshape(batch, seq * n_heads, HEAD_W),
                           v_f.reshape(batch, seq * n_heads, HEAD_W),
                           cache_k, cache_v, page_table, layer,
                           lam_params, subln_g[layer], lam_init).reshape(m, qk_w)
    else:
        tb = _tile(seq, PROMPT_ATTN_BLOCK)
        (q_b,) = proj(off=offs["q"], rope=True, scale=scale, out_bf16=True)
        k_f, k_b = proj(off=offs["k"], rope=True, f32_layout="heads", out_bf16=True)
        v_f, v_t = proj(off=offs["v"], rope=False, f32_layout="heads", transposed_tk=tb, seq=seq)
        t_b = _attn_prompt(q_b, k_b, v_t, lam_params, subln_g[layer], lam_init, batch, seq, tb)

    c_b, cc = _proj_cbranch(h, w_in, layer, offs["c_b"], offs["c_c"], offs["c_x"], d_c, tm)
    cc3 = cc.reshape(batch, seq, d_c)
    if is_sample:
        prev_c = _pad_history(group["state_c"][layer], V7X_SUBLANES)
        new_c = jnp.concatenate([group["state_c"][layer], cc3], axis=1)[:, -(taps_c - 1):]
    else:
        prev_c = jnp.zeros((batch, V7X_SUBLANES, d_c), F32)
        new_c = cc3[:, seq - (taps_c - 1):]
    t_c = _conv_mul(cc3, prev_c, conv_c_w[layer], c_b.reshape(batch, seq, d_c),
                    group["tt"]).reshape(m, d_c)

    merged = _merge(t_a, t_b, t_c, h, w_a_out, w_b_out, w_c_out, w_in, layer,
                    (offs["g_a"], offs["g_b"], offs["g_c"]), tm)
    x_mid, h2 = _wo_residual_norm(x, merged, w_o, layer, norm_ffn[layer], group["tm_ffn"])

    taps_f = conv_ffn_w.shape[1]
    if is_sample:
        prev_f = _pad_history(group["state_f"][layer], V7X_SUBLANES)
        x_new, h_next, g_up = _ffn(h2, x_mid, w_ffn_in, conv_ffn_w, w_ffn_down, layer, g_next,
                                   prev_f, seq, tm, final)
        new_f = jnp.concatenate([group["state_f"][layer], g_up.reshape(batch, seq, d_ff)],
                                axis=1)[:, -(taps_f - 1):]
    else:
        x_new, h_next, g_tail = _ffn(h2, x_mid, w_ffn_in, conv_ffn_w, w_ffn_down, layer, g_next,
                                     None, seq, group["tm_ffn"], final)
        blocks_per_seq = seq // group["tm_ffn"]
        g_tail = g_tail.reshape(batch, blocks_per_seq, V7X_SUBLANES, d_ff)
        new_f = g_tail[:, blocks_per_seq - 1, V7X_SUBLANES - (taps_f - 1):]
    k_out = k_f.reshape(batch, seq, n_heads, HEAD_W)
    v_out = v_f.reshape(batch, seq, n_heads, HEAD_W)
    return x_new, h_next, k_out, v_out, new_a, new_c, new_f


def kernel(x_prompt, x_sample, cache_k, cache_v, state_conv_a, state_conv_c, state_conv_ffn,
           page_table, norm_mix, w_in, conv_a_w, conv_a_b, ln_a_g, ln_a_b, w_a_out, lam_q1,
           lam_k1, lam_q2, lam_k2, subln_g, w_b_out, conv_c_w, w_c_out, w_o, norm_ffn, w_ffn_in,
           conv_ffn_w, w_ffn_down, norm_final):
    bp, t_p, d = x_prompt.shape
    bs, t_s, _ = x_sample.shape
    depth = w_in.shape[0]
    past = page_table.shape[1] * PAGE_SIZE

    tm_p = _tile(t_p, 1024)
    cos_p, sin_p = _rope_tables(jnp.arange(t_p, dtype=jnp.int32))
    cos_s, sin_s = _rope_tables(past + jnp.arange(t_s, dtype=jnp.int32))
    groups = [
        dict(batch=bp, seq=t_p, tm=tm_p, tm_ffn=_tile(t_p, 512), tt=_tile(t_p, 256),
             rope=(cos_p, sin_p), cache=None),
        dict(batch=bs, seq=t_s, tm=bs * t_s, tm_ffn=bs * t_s, tt=t_s,
             rope=(jnp.tile(cos_s, (bs, 1)), jnp.tile(sin_s, (bs, 1))),
             cache=(cache_k, cache_v, page_table),
             state_a=state_conv_a, state_c=state_conv_c, state_f=state_conv_ffn),
    ]
    w_o, w_ffn_in, w_ffn_down = (w.astype(BF16) for w in (w_o, w_ffn_in, w_ffn_down))
    xs = [x_prompt.reshape(bp * t_p, d), x_sample.reshape(bs * t_s, d)]
    hs = [_rmsnorm_bf16(x, norm_mix[0]) for x in xs]
    collected = [[], []]
    for layer in range(depth):
        lam_init = 0.8 - 0.6 * math.exp(-0.3 * layer)
        final = layer == depth - 1
        g_next = norm_final if final else norm_mix[layer + 1]
        weights = (w_in, conv_a_w, conv_a_b, ln_a_g, ln_a_b, w_a_out, lam_q1, lam_k1, lam_q2,
                   lam_k2, subln_g, w_b_out, conv_c_w, w_c_out, w_o, norm_ffn, w_ffn_in,
                   conv_ffn_w, w_ffn_down, g_next)
        for gi, group in enumerate(groups):
            x_new, h_next, k_out, v_out, new_a, new_c, new_f = _layer(
                xs[gi], hs[gi], layer, lam_init, final, group, weights)
            xs[gi], hs[gi] = x_new, h_next
            collected[gi].append((k_out, v_out, new_a, new_c, new_f))
    y_prompt = xs[0].reshape(bp, t_p, d)
    y_sample = xs[1].reshape(bs, t_s, d)
    stack = lambda gi, idx: jnp.stack([c[idx] for c in collected[gi]])
    return (y_prompt, y_sample,
            stack(0, 0), stack(0, 1), stack(0, 2), stack(0, 3), stack(0, 4),
            stack(1, 0), stack(1, 1), stack(1, 2), stack(1, 3), stack(1, 4))
```

```python
import functools
import math

import jax
import jax.numpy as jnp
from jax import lax
from jax.experimental import pallas as pl
from jax.experimental.pallas import tpu as pltpu

BF16 = jnp.bfloat16
F32 = jnp.float32

HEAD_DIM = 64
PAGE_SIZE = 128
ROPE_THETA = 10000.0
RMS_EPS = 1e-6
LN_EPS = 1e-5
SUBLN_EPS = 1e-5
NEG_INF = -1e30

V7X_LANES = 128
V7X_SUBLANES = 8
V7X_VMEM_LIMIT_BYTES = 60 * 1024 * 1024

HEAD_W = 2 * HEAD_DIM


def _tile(dim, pref):
    if dim <= pref:
        return dim
    t = pref
    while t >= V7X_SUBLANES:
        if dim % t == 0 and t % V7X_SUBLANES == 0:
            return t
        t -= V7X_SUBLANES
    return dim


def _tile_count(total, pref):
    return max(c for c in range(1, min(total, pref) + 1) if total % c == 0)


SAMPLE_PAGES_PER_STEP = 8
PROMPT_ATTN_BLOCK = 512


def _params(sem):
    return pltpu.CompilerParams(dimension_semantics=sem,
                                vmem_limit_bytes=V7X_VMEM_LIMIT_BYTES)


def _dot(a, b):
    return jnp.dot(a, b, preferred_element_type=F32)


def _dot_nt(a, b):
    return lax.dot_general(a, b, (((1,), (1,)), ((), ())), preferred_element_type=F32)


def _rms(x, g, eps):
    return x * lax.rsqrt(jnp.mean(x * x, axis=-1, keepdims=True) + eps) * g


def _rmsnorm_kernel(x_ref, g_ref, o_ref):
    o_ref[...] = _rms(x_ref[...], g_ref[...], RMS_EPS).astype(o_ref.dtype)


def _rmsnorm_bf16(x, g):
    m, d = x.shape
    tm = _tile(m, 512)
    return pl.pallas_call(
        _rmsnorm_kernel,
        grid=(m // tm,),
        in_specs=[pl.BlockSpec((tm, d), lambda i: (i, 0)),
                  pl.BlockSpec((1, d), lambda i: (0, 0))],
        out_specs=pl.BlockSpec((tm, d), lambda i: (i, 0)),
        out_shape=jax.ShapeDtypeStruct((m, d), BF16),
        compiler_params=_params(("parallel",)),
        name="rmsnorm_in",
    )(x, g.reshape(1, d))


def _w_spec(d, tn, layer, col_off):
    assert col_off % tn == 0
    base = col_off // tn
    return pl.BlockSpec((None, d, tn), lambda i, j: (layer, 0, base + j))


def _glu_kernel(h_ref, wv_ref, wg_ref, o_ref):
    h = h_ref[...]
    val = _dot(h, wv_ref[...].astype(BF16))
    gate = _dot(h, wg_ref[...].astype(BF16))
    o_ref[...] = val * jax.nn.sigmoid(gate)


def _proj_glu(h, w_in, layer, off_val, off_gate, width, tm):
    m, d = h.shape
    tn = _tile(width, 512)
    return pl.pallas_call(
        _glu_kernel,
        grid=(m // tm, width // tn),
        in_specs=[pl.BlockSpec((tm, d), lambda i, j: (i, 0)),
                  _w_spec(d, tn, layer, off_val),
                  _w_spec(d, tn, layer, off_gate)],
        out_specs=pl.BlockSpec((tm, tn), lambda i, j: (i, j)),
        out_shape=jax.ShapeDtypeStruct((m, width), F32),
        compiler_params=_params(("parallel", "arbitrary")),
        name="proj_glu",
    )(h, w_in, w_in)


def _rope_tile(x, cos, sin_signed):
    lane = lax.broadcasted_iota(jnp.int32, x.shape, 1)
    first_half = (lane % HEAD_DIM) < (HEAD_DIM // 2)
    partner = jnp.where(first_half,
                        pltpu.roll(x, V7X_LANES - HEAD_DIM // 2, 1),
                        pltpu.roll(x, HEAD_DIM // 2, 1))
    return x * cos + partner * sin_signed


def _heads_kernel(h_ref, w_ref, cos_ref, sin_ref, *o_refs, rope, scale, f32_layout, out_bf16,
                  transposed_tk):
    acc = _dot(h_ref[...], w_ref[...].astype(BF16))
    tm, width = acc.shape
    n_heads = width // HEAD_W
    for head in range(n_heads):
        sl = slice(head * HEAD_W, (head + 1) * HEAD_W)
        y = acc[:, sl]
        if rope:
            y = _rope_tile(y, cos_ref[...], sin_ref[...])
        if scale != 1.0:
            y = y * scale
        k = 0
        if f32_layout == "rows":
            o_refs[k][:, sl] = y
            k += 1
        elif f32_layout == "heads":
            o_refs[k][pl.ds(head, tm, stride=n_heads), :] = y
            k += 1
        if out_bf16:
            o_refs[k][:, sl] = y.astype(BF16)
            k += 1
        if transposed_tk:
            for c in range(tm // transposed_tk):
                rows = slice(c * transposed_tk, (c + 1) * transposed_tk)
                o_refs[k][head, c] = y[rows, :].T.astype(BF16)


def _proj_heads(h, w_in, layer, off, width, cos, sin, tm, *, rope, scale=1.0, f32_layout=None,
                out_bf16=False, transposed_tk=0, seq=None):
    m, d = h.shape
    n_heads = width // HEAD_W
    n_pos_blocks = cos.shape[0] // tm
    tab_spec = pl.BlockSpec((tm, V7X_LANES), lambda i: (i % n_pos_blocks, 0))
    out_specs, out_shape = [], []
    if f32_layout == "rows":
        out_specs.append(pl.BlockSpec((tm, width), lambda i: (i, 0)))
        out_shape.append(jax.ShapeDtypeStruct((m, width), F32))
    elif f32_layout == "heads":
        out_specs.append(pl.BlockSpec((tm * n_heads, HEAD_W), lambda i: (i, 0)))
        out_shape.append(jax.ShapeDtypeStruct((m * n_heads, HEAD_W), F32))
    if out_bf16:
        out_specs.append(pl.BlockSpec((tm, width), lambda i: (i, 0)))
        out_shape.append(jax.ShapeDtypeStruct((m, width), BF16))
    if transposed_tk:
        assert tm % transposed_tk == 0 and seq % tm == 0
        blocks_per_seq = seq // tm
        sub = tm // transposed_tk
        out_specs.append(pl.BlockSpec(
            (None, n_heads, sub, HEAD_W, transposed_tk),
            lambda i: (i // blocks_per_seq, 0, i % blocks_per_seq, 0, 0)))
        out_shape.append(jax.ShapeDtypeStruct(
            (m // seq, n_heads, seq // transposed_tk, HEAD_W, transposed_tk), BF16))
    assert off % width == 0
    return pl.pallas_call(
        functools.partial(_heads_kernel, rope=rope, scale=scale, f32_layout=f32_layout,
                          out_bf16=out_bf16, transposed_tk=transposed_tk),
        grid=(m // tm,),
        in_specs=[pl.BlockSpec((tm, d), lambda i: (i, 0)),
                  pl.BlockSpec((None, d, width), lambda i: (layer, 0, off // width)),
                  tab_spec, tab_spec],
        out_specs=out_specs,
        out_shape=out_shape,
        compiler_params=_params(("parallel",)),
        name="proj_heads",
    )(h, w_in, cos, sin)


def _cbranch_kernel(h_ref, wb_ref, wc_ref, wx_ref, cb_ref, cc_ref):
    h = h_ref[...]
    cb_ref[...] = _dot(h, wb_ref[...].astype(BF16)).astype(cb_ref.dtype)
    c_c = _dot(h, wc_ref[...].astype(BF16))
    c_x = _dot(h, wx_ref[...].astype(BF16))
    cc_ref[...] = c_c * c_x


def _proj_cbranch(h, w_in, layer, off_b, off_c, off_x, width, tm):
    m, d = h.shape
    tn = _tile(width, 512)
    return pl.pallas_call(
        _cbranch_kernel,
        grid=(m // tm, width // tn),
        in_specs=[pl.BlockSpec((tm, d), lambda i, j: (i, 0)),
                  _w_spec(d, tn, layer, off_b),
                  _w_spec(d, tn, layer, off_c),
                  _w_spec(d, tn, layer, off_x)],
        out_specs=[pl.BlockSpec((tm, tn), lambda i, j: (i, j)),
                   pl.BlockSpec((tm, tn), lambda i, j: (i, j))],
        out_shape=[jax.ShapeDtypeStruct((m, width), BF16),
                   jax.ShapeDtypeStruct((m, width), F32)],
        compiler_params=_params(("parallel", "arbitrary")),
        name="proj_cbranch",
    )(h, w_in, w_in, w_in)


def _dwconv_fill(u_ref, prev_ref, buf, hist_rows, tt):
    ti = pl.program_id(1)

    @pl.when(ti == 0)
    def _():
        buf[0:hist_rows, :] = prev_ref[...]

    @pl.when(ti > 0)
    def _():
        buf[0:hist_rows, :] = buf[tt:tt + hist_rows, :]

    buf[hist_rows:hist_rows + tt, :] = u_ref[...]


def _dwconv_rows(buf, w_ref, taps, hist_rows, r0, rows, c0, cols):
    base = hist_rows - (taps - 1)
    acc = None
    for k in range(taps):
        term = buf[r0 + base + k:r0 + base + k + rows, c0:c0 + cols] * w_ref[k:k + 1, c0:c0 + cols]
        acc = term if acc is None else acc + term
    return acc


def _conv_ln_silu_kernel(u_ref, prev_ref, w_ref, b_ref, g_ref, beta_ref, o_ref, buf, shifted,
                         ybuf, *, taps, hist_rows, tt, row_chunk, col_chunk):
    _dwconv_fill(u_ref, prev_ref, buf, hist_rows, tt)
    c = u_ref.shape[-1]
    base = hist_rows - (taps - 1)
    span = shifted.shape[1]
    for res in range(1, V7X_SUBLANES):
        shifted[res - 1] = buf[res:res + span, :]
    for r0 in range(0, tt, row_chunk):
        for c0 in range(0, c, col_chunk):
            acc = None
            for k in range(taps):
                res, tile_off = (base + k) % V7X_SUBLANES, (base + k) // V7X_SUBLANES * V7X_SUBLANES
                rows = slice(r0 + tile_off, r0 + tile_off + row_chunk)
                cols = slice(c0, c0 + col_chunk)
                window = buf[rows, cols] if res == 0 else shifted[res - 1, rows, cols]
                term = window * w_ref[k:k + 1, cols]
                acc = term if acc is None else acc + term
            ybuf[r0:r0 + row_chunk, c0:c0 + col_chunk] = acc
    y = ybuf[...] + b_ref[...]
    mu = jnp.mean(y, axis=-1, keepdims=True)
    yc = y - mu
    var = jnp.mean(yc * yc, axis=-1, keepdims=True)
    z = yc * lax.rsqrt(var + LN_EPS) * g_ref[...] + beta_ref[...]
    o_ref[...] = (z * jax.nn.sigmoid(z)).astype(o_ref.dtype)


def _conv_ln_silu(u, prev, w, bias, ln_g, ln_b, tt):
    b, t, c = u.shape
    taps = w.shape[0]
    hist_rows = prev.shape[1]
    row_chunk = min(tt, 32)
    col_chunk = min(c, 512)
    vec = lambda a: a.reshape(1, c)
    return pl.pallas_call(
        functools.partial(_conv_ln_silu_kernel, taps=taps, hist_rows=hist_rows, tt=tt,
                          row_chunk=row_chunk, col_chunk=col_chunk),
        grid=(b, t // tt),
        in_specs=[pl.BlockSpec((None, tt, c), lambda bi, ti: (bi, ti, 0)),
                  pl.BlockSpec((None, hist_rows, c), lambda bi, ti: (bi, 0, 0)),
                  pl.BlockSpec((taps, c), lambda bi, ti: (0, 0)),
                  pl.BlockSpec((1, c), lambda bi, ti: (0, 0)),
                  pl.BlockSpec((1, c), lambda bi, ti: (0, 0)),
                  pl.BlockSpec((1, c), lambda bi, ti: (0, 0))],
        out_specs=pl.BlockSpec((None, tt, c), lambda bi, ti: (bi, ti, 0)),
        out_shape=jax.ShapeDtypeStruct((b, t, c), BF16),
        scratch_shapes=[pltpu.VMEM((hist_rows + tt, c), F32),
                        pltpu.VMEM((V7X_SUBLANES - 1, tt + hist_rows - V7X_SUBLANES, c), F32),
                        pltpu.VMEM((tt, c), F32)],
        compiler_params=_params(("parallel", "arbitrary")),
        name="conv_a",
    )(u, prev, w, vec(bias), vec(ln_g), vec(ln_b))


def _conv_mul_kernel(u_ref, prev_ref, w_ref, cb_ref, o_ref, buf, *, taps, hist_rows, tt):
    _dwconv_fill(u_ref, prev_ref, buf, hist_rows, tt)
    c = u_ref.shape[-1]
    conv = _dwconv_rows(buf, w_ref, taps, hist_rows, 0, tt, 0, c)
    o_ref[...] = (cb_ref[...].astype(F32) * conv).astype(o_ref.dtype)


def _conv_mul(u, prev, w, cb, tt):
    b, t, c = u.shape
    taps = w.shape[0]
    hist_rows = prev.shape[1]
    return pl.pallas_call(
        functools.partial(_conv_mul_kernel, taps=taps, hist_rows=hist_rows, tt=tt),
        grid=(b, t // tt),
        in_specs=[pl.BlockSpec((None, tt, c), lambda bi, ti: (bi, ti, 0)),
                  pl.BlockSpec((None, hist_rows, c), lambda bi, ti: (bi, 0, 0)),
                  pl.BlockSpec((taps, c), lambda bi, ti: (0, 0)),
                  pl.BlockSpec((None, tt, c), lambda bi, ti: (bi, ti, 0))],
        out_specs=pl.BlockSpec((None, tt, c), lambda bi, ti: (bi, ti, 0)),
        out_shape=jax.ShapeDtypeStruct((b, t, c), BF16),
        scratch_shapes=[pltpu.VMEM((hist_rows + tt, c), F32)],
        compiler_params=_params(("parallel", "arbitrary")),
        name="conv_c",
    )(u, prev, w, cb)


def _lambda(lq1_ref, lk1_ref, lq2_ref, lk2_ref, lam_init):
    s1 = jnp.sum(lq1_ref[...] * lk1_ref[...], axis=-1, keepdims=True)
    s2 = jnp.sum(lq2_ref[...] * lk2_ref[...], axis=-1, keepdims=True)
    return jnp.exp(s1) - jnp.exp(s2) + lam_init


def _subln(o, g, lam_init):
    return _rms(o, g, SUBLN_EPS) * (1.0 - lam_init)


def _attn_prompt_kernel(q_ref, k_ref, vt_ref, lq1_ref, lk1_ref, lq2_ref, lk2_ref, gs_ref,
                        o_ref, s_ref, m_ref, l_ref, acc_ref, *, tb, lam_init):
    qi = pl.program_id(2)
    q = q_ref[...]
    lane = lax.broadcasted_iota(jnp.int32, q.shape, 1)
    zero = jnp.zeros_like(q)
    q_maps = (jnp.where(lane < HEAD_DIM, q, zero), jnp.where(lane >= HEAD_DIM, q, zero))
    m_ref[...] = jnp.full(m_ref.shape, NEG_INF, F32)
    l_ref[...] = jnp.zeros(l_ref.shape, F32)
    acc_ref[...] = jnp.zeros(acc_ref.shape, F32)

    def scores(kj, slot):
        k = k_ref[pl.ds(pl.multiple_of(kj * tb, tb), tb), :]
        for mp in range(2):
            s_ref[slot, mp] = _dot_nt(k, q_maps[mp])

    def consume(kj, slot, masked):
        vt = vt_ref[kj]
        for mp in range(2):
            s = s_ref[slot, mp]
            if masked:
                key = lax.broadcasted_iota(jnp.int32, s.shape, 0)
                qry = lax.broadcasted_iota(jnp.int32, s.shape, 1)
                s = jnp.where(key <= qry, s, NEG_INF)
            m_prev = m_ref[mp]
            m_new = jnp.maximum(m_prev, jnp.max(s, axis=0, keepdims=True))
            alpha = jnp.exp(m_prev - m_new)
            p = jnp.exp(s - m_new)
            l_ref[mp] = alpha * l_ref[mp] + jnp.sum(p, axis=0, keepdims=True)
            acc_ref[mp] = alpha * acc_ref[mp] + _dot(vt, p.astype(BF16))
            m_ref[mp] = m_new

    scores(0, 0)

    def block_pair(t, carry):
        scores(2 * t + 1, 1)
        consume(2 * t, 0, False)
        scores(2 * t + 2, 0)
        consume(2 * t + 1, 1, False)
        return carry

    lax.fori_loop(0, qi // 2, block_pair, 0)

    @pl.when(qi % 2 == 0)
    def _():
        consume(qi, 0, True)

    @pl.when(qi % 2 == 1)
    def _():
        scores(qi, 1)
        consume(qi - 1, 0, False)
        consume(qi, 1, True)

    lam = _lambda(lq1_ref, lk1_ref, lq2_ref, lk2_ref, lam_init)
    diff = acc_ref[0] / l_ref[0] - lam * (acc_ref[1] / l_ref[1])
    ms = jnp.mean(diff * diff, axis=0, keepdims=True)
    normed = diff * lax.rsqrt(ms + SUBLN_EPS) * gs_ref[...] * (1.0 - lam_init)
    o_ref[...] = normed.T.astype(o_ref.dtype)


def _attn_prompt(qb, kb, vt, lam_params, g_sub, lam_init, batch, seq, tb):
    m, w = qb.shape
    n_heads = w // HEAD_W
    nq = seq // tb
    small = lambda n: pl.BlockSpec((1, n), lambda b, h, qi: (0, 0))
    return pl.pallas_call(
        functools.partial(_attn_prompt_kernel, tb=tb, lam_init=lam_init),
        grid=(batch, n_heads, nq),
        in_specs=[pl.BlockSpec((tb, HEAD_W), lambda b, h, qi: (b * nq + qi, h)),
                  pl.BlockSpec((seq, HEAD_W), lambda b, h, qi: (b, h)),
                  pl.BlockSpec((None, None, nq, HEAD_W, tb), lambda b, h, qi: (b, h, 0, 0, 0)),
                  small(HEAD_DIM), small(HEAD_DIM), small(HEAD_DIM), small(HEAD_DIM),
                  pl.BlockSpec((HEAD_W, 1), lambda b, h, qi: (0, 0))],
        out_specs=pl.BlockSpec((tb, HEAD_W), lambda b, h, qi: (b * nq + qi, h)),
        out_shape=jax.ShapeDtypeStruct((m, w), BF16),
        scratch_shapes=[pltpu.VMEM((2, 2, tb, tb), F32),
                        pltpu.VMEM((2, 1, tb), F32),
                        pltpu.VMEM((2, 1, tb), F32),
                        pltpu.VMEM((2, HEAD_W, tb), F32)],
        compiler_params=_params(("parallel", "parallel", "arbitrary")),
        name="attn_prompt",
    )(qb, kb, vt, *lam_params, g_sub.reshape(HEAD_W, 1))


def _attn_sample_kernel(pt_ref, q_ref, kn_ref, vn_ref, *refs, t_new, n_heads, lam_init, n_pages):
    page_refs = refs[:2 * n_pages]
    (lq1_ref, lk1_ref, lq2_ref, lk2_ref, gs_ref, o_ref,
     qbd_ref, bias_ref, m_ref, l_ref, acc_ref) = refs[2 * n_pages:]
    del pt_ref
    p = pl.program_id(1)
    rows_per_head = 2 * t_new

    @pl.when(p == 0)
    def _():
        q = q_ref[...]
        lane = lax.broadcasted_iota(jnp.int32, (t_new, HEAD_W), 1)
        for h in range(n_heads):
            qh = q[:, h * HEAD_W:(h + 1) * HEAD_W]
            r0 = h * rows_per_head
            qbd_ref[r0:r0 + t_new, :] = jnp.where(lane < HEAD_DIM, qh, 0.0)
            qbd_ref[r0 + t_new:r0 + rows_per_head, :] = jnp.where(lane >= HEAD_DIM, qh, 0.0)
        m_ref[...] = jnp.full(m_ref.shape, NEG_INF, F32)
        l_ref[...] = jnp.zeros(l_ref.shape, F32)
        acc_ref[...] = jnp.zeros(acc_ref.shape, F32)

        row = lax.broadcasted_iota(jnp.int32, bias_ref.shape, 0)
        col = lax.broadcasted_iota(jnp.int32, bias_ref.shape, 1)
        bias_ref[...] = jnp.where(col % n_heads == row // rows_per_head, 0.0, NEG_INF)

    def update(scores, values):
        m_prev = m_ref[...]
        m_new = m_prev
        for s in scores:
            m_new = jnp.maximum(m_new, jnp.max(s, axis=-1, keepdims=True))
        alpha = jnp.exp(m_prev - m_new)
        l_new = alpha * l_ref[...]
        acc = alpha * acc_ref[...]
        for s, v in zip(scores, values):
            pr = jnp.exp(s - m_new)
            l_new = l_new + jnp.sum(pr, axis=-1, keepdims=True)
            acc = acc + _dot(pr.astype(BF16), v.astype(BF16))
        l_ref[...] = l_new
        acc_ref[...] = acc
        m_ref[...] = m_new

    qbd = qbd_ref[...].astype(BF16)
    page_rows = PAGE_SIZE * n_heads
    bias = bias_ref[...]
    scores = [_dot_nt(qbd, k_ref[...].reshape(page_rows, HEAD_W).astype(BF16)) + bias
              for k_ref in page_refs[:n_pages]]
    values = [v_ref[...].reshape(page_rows, HEAD_W) for v_ref in page_refs[n_pages:]]
    update(scores, values)

    @pl.when(p == pl.num_programs(1) - 1)
    def _():
        s = _dot_nt(qbd, kn_ref[...].astype(BF16))
        row = lax.broadcasted_iota(jnp.int32, s.shape, 0)
        col = lax.broadcasted_iota(jnp.int32, s.shape, 1)
        valid = jnp.logical_and(col % n_heads == row // rows_per_head,
                                col // n_heads <= row % t_new)
        update([jnp.where(valid, s, NEG_INF)], [vn_ref[...]])
        lam = _lambda(lq1_ref, lk1_ref, lq2_ref, lk2_ref, lam_init)
        o = acc_ref[...] / l_ref[...]
        for h in range(n_heads):
            r0 = h * rows_per_head
            diff = o[r0:r0 + t_new, :] - lam * o[r0 + t_new:r0 + rows_per_head, :]
            o_ref[:, h * HEAD_W:(h + 1) * HEAD_W] = _subln(
                diff, gs_ref[...], lam_init).astype(o_ref.dtype)


def _attn_sample(q, k_new, v_new, cache_k, cache_v, page_table, layer, lam_params, g_sub,
                 lam_init):
    b, t_new, w = q.shape
    n_heads = w // HEAD_W
    total_pages = page_table.shape[1]
    n_pages = _tile_count(total_pages, SAMPLE_PAGES_PER_STEP)
    assert t_new % V7X_SUBLANES == 0
    rows = 2 * n_heads * t_new

    q_spec = pl.BlockSpec((None, t_new, w), lambda bi, p, pt: (bi, 0, 0))
    new_spec = pl.BlockSpec((None, t_new * n_heads, HEAD_W), lambda bi, p, pt: (bi, 0, 0))
    page_shape = (None, None, PAGE_SIZE, n_heads, HEAD_W)
    page_specs = [
        pl.BlockSpec(page_shape, lambda bi, p, pt, c=c: (layer, pt[bi, n_pages * p + c], 0, 0, 0))
        for c in range(n_pages)]
    small = lambda n: pl.BlockSpec((1, n), lambda bi, p, pt: (0, 0))
    grid_spec = pltpu.PrefetchScalarGridSpec(
        num_scalar_prefetch=1,
        grid=(b, total_pages // n_pages),
        in_specs=[q_spec, new_spec, new_spec] + page_specs + page_specs
                 + [small(HEAD_DIM), small(HEAD_DIM), small(HEAD_DIM), small(HEAD_DIM),
                    small(HEAD_W)],
        out_specs=pl.BlockSpec((None, t_new, w), lambda bi, p, pt: (bi, 0, 0)),
        scratch_shapes=[pltpu.VMEM((rows, HEAD_W), F32),
                        pltpu.VMEM((rows, PAGE_SIZE * n_heads), F32),
                        pltpu.VMEM((rows, 1), F32),
                        pltpu.VMEM((rows, 1), F32),
                        pltpu.VMEM((rows, HEAD_W), F32)],
    )
    return pl.pallas_call(
        functools.partial(_attn_sample_kernel, t_new=t_new, n_heads=n_heads, lam_init=lam_init,
                          n_pages=n_pages),
        grid_spec=grid_spec,
        out_shape=jax.ShapeDtypeStruct((b, t_new, w), BF16),
        compiler_params=_params(("parallel", "arbitrary")),
        name="attn_sample",
    )(page_table, q, k_new, v_new, *([cache_k] * n_pages), *([cache_v] * n_pages), *lam_params,
      g_sub.reshape(1, HEAD_W))


def _merge_kernel(ta_ref, tb_ref, tc_ref, h_ref, wa_ref, wb_ref, wc_ref,
                  wga_ref, wgb_ref, wgc_ref, o_ref):
    h = h_ref[...]

    def branch(t_ref, w_ref, wg_ref):
        y = _dot(t_ref[...], w_ref[...].astype(BF16))
        g = _dot(h, wg_ref[...].astype(BF16))
        return jax.nn.sigmoid(g) * y

    merged = (branch(ta_ref, wa_ref, wga_ref) + branch(tb_ref, wb_ref, wgb_ref)
              + branch(tc_ref, wc_ref, wgc_ref))
    o_ref[...] = merged.astype(o_ref.dtype)


def _merge(ta, tb, tc, h, w_a_out, w_b_out, w_c_out, w_in, layer, gate_offs, tm):
    m, d = h.shape
    tn = _tile(d, 256)
    act = lambda a: pl.BlockSpec((tm, a.shape[1]), lambda i, j: (i, 0))
    wout = lambda a: pl.BlockSpec((None, a.shape[1], tn), lambda i, j: (layer, 0, j))
    return pl.pallas_call(
        _merge_kernel,
        grid=(m // tm, d // tn),
        in_specs=[act(ta), act(tb), act(tc), act(h),
                  wout(w_a_out), wout(w_b_out), wout(w_c_out),
                  _w_spec(d, tn, layer, gate_offs[0]),
                  _w_spec(d, tn, layer, gate_offs[1]),
                  _w_spec(d, tn, layer, gate_offs[2])],
        out_specs=pl.BlockSpec((tm, tn), lambda i, j: (i, j)),
        out_shape=jax.ShapeDtypeStruct((m, d), BF16),
        compiler_params=_params(("parallel", "arbitrary")),
        name="merge",
    )(ta, tb, tc, h, w_a_out, w_b_out, w_c_out, w_in, w_in, w_in)


def _residual_kernel(x_ref, lhs_ref, w_ref, g_ref, xo_ref, *h_refs, col_chunk, final):
    k = pl.program_id(1)

    @pl.when(k == 0)
    def _():
        xo_ref[...] = x_ref[...]

    lhs = lhs_ref[...]
    for c0 in range(0, xo_ref.shape[1], col_chunk):
        xo_ref[:, c0:c0 + col_chunk] += _dot(lhs, w_ref[:, c0:c0 + col_chunk])

    @pl.when(k == pl.num_programs(1) - 1)
    def _():
        y = _rms(xo_ref[...], g_ref[...], RMS_EPS)
        if final:
            xo_ref[...] = y
        else:
            h_refs[0][...] = y.astype(h_refs[0].dtype)


def _residual_matmul_norm(x, lhs, w, layer, g_next, tm, final, name):
    m, d = x.shape
    kdim = lhs.shape[1]
    tk = _tile(kdim, 512)
    row_spec = pl.BlockSpec((tm, d), lambda i, k: (i, 0))
    out_specs = [row_spec] + ([] if final else [row_spec])
    out_shape = ([jax.ShapeDtypeStruct((m, d), F32)]
                 + ([] if final else [jax.ShapeDtypeStruct((m, d), BF16)]))
    outs = pl.pallas_call(
        functools.partial(_residual_kernel, col_chunk=_tile(d, 512), final=final),
        grid=(m // tm, kdim // tk),
        in_specs=[row_spec,
                  pl.BlockSpec((tm, tk), lambda i, k: (i, k)),
                  pl.BlockSpec((None, tk, d), lambda i, k: (layer, k, 0)),
                  pl.BlockSpec((1, d), lambda i, k: (0, 0))],
        out_specs=out_specs,
        out_shape=out_shape,
        compiler_params=_params(("parallel", "arbitrary")),
        name=name,
    )(x, lhs, w, g_next.reshape(1, d))
    return (outs[0], None) if final else tuple(outs)


def _ffn_up_kernel(h_ref, wg_ref, wv_ref, wc_ref, prev_ref, act_ref, gt_ref, buf_ref, hist_ref,
                   *, stream, seq_len, blocks_per_seq, sub):
    i = pl.program_id(0)
    j = pl.program_id(1)
    h = h_ref[...]
    tm = h.shape[0]
    fc = act_ref.shape[1]
    hist = V7X_SUBLANES
    nseq = tm // seq_len

    if stream:
        @pl.when(i % blocks_per_seq == 0)
        def _():
            buf_ref[:, 0:hist, :] = jnp.zeros((1, hist, fc), F32)

        @pl.when(i % blocks_per_seq != 0)
        def _():
            buf_ref[:, 0:hist, :] = hist_ref[j]
    else:
        buf_ref[:, 0:hist, :] = prev_ref[...]

    for c0 in range(0, fc, sub):
        cols = slice(c0, c0 + sub)
        gate = _dot(h, wg_ref[:, cols])
        val = _dot(h, wv_ref[:, cols])
        if stream:
            hist_ref[j, :, :, cols] = gate[tm - hist:tm, :].reshape(1, hist, sub)
            gt_ref[:, cols] = gate[tm - hist:tm, :]
        else:
            gt_ref[:, cols] = gate
        buf_ref[:, hist:hist + seq_len, cols] = gate.reshape(nseq, seq_len, sub)
        w = wc_ref[:, cols]
        conv = (buf_ref[:, hist - 2:hist - 2 + seq_len, cols] * w[0:1, :].reshape(1, 1, sub)
                + buf_ref[:, hist - 1:hist - 1 + seq_len, cols] * w[1:2, :].reshape(1, 1, sub)
                + buf_ref[:, hist:hist + seq_len, cols] * w[2:3, :].reshape(1, 1, sub))
        conv = conv.reshape(tm, sub)
        act_ref[:, cols] = (conv * jax.nn.sigmoid(conv) * val).astype(act_ref.dtype)


def _ffn_up(h2, w_ffn_in, conv_w, layer, prev, seq_len, tm):
    m, d = h2.shape
    dff = conv_w.shape[-1]
    fc = _tile(dff, 512)
    nj = dff // fc
    stream = prev is None
    if stream:
        assert seq_len % tm == 0
        blocks_per_seq = seq_len // tm
        rows_in_block = tm
        prev = jnp.zeros((1, V7X_SUBLANES, dff), F32)
        prev_spec = pl.BlockSpec((1, V7X_SUBLANES, fc), lambda i, j: (0, 0, j))
        gt_shape = jax.ShapeDtypeStruct((m // tm, V7X_SUBLANES, dff), F32)
        gt_spec = pl.BlockSpec((None, V7X_SUBLANES, fc), lambda i, j: (i, 0, j))
        nseq = 1
    else:
        assert tm % seq_len == 0 and m == tm
        blocks_per_seq = 1
        rows_in_block = seq_len
        nseq = tm // seq_len
        prev_spec = pl.BlockSpec((nseq, V7X_SUBLANES, fc), lambda i, j: (0, 0, j))
        gt_shape = jax.ShapeDtypeStruct((m, dff), F32)
        gt_spec = pl.BlockSpec((tm, fc), lambda i, j: (i, j))
    return pl.pallas_call(
        functools.partial(_ffn_up_kernel, stream=stream, seq_len=rows_in_block,
                          blocks_per_seq=blocks_per_seq,
                          sub=fc // 2 if fc % (4 * V7X_LANES) == 0 else fc),
        grid=(m // tm, nj),
        in_specs=[pl.BlockSpec((tm, d), lambda i, j: (i, 0)),
                  pl.BlockSpec((None, d, fc), lambda i, j: (layer, 0, j)),
                  pl.BlockSpec((None, d, fc), lambda i, j: (layer, 0, nj + j)),
                  pl.BlockSpec((None, conv_w.shape[1], fc), lambda i, j: (layer, 0, j)),
                  prev_spec],
        out_specs=[pl.BlockSpec((tm, fc), lambda i, j: (i, j)), gt_spec],
        out_shape=[jax.ShapeDtypeStruct((m, dff), BF16), gt_shape],
        scratch_shapes=[pltpu.VMEM((nseq, V7X_SUBLANES + rows_in_block, fc), F32),
                        pltpu.VMEM((nj, 1, V7X_SUBLANES, fc), F32)],
        compiler_params=_params(("arbitrary", "arbitrary")),
        name="ffn_up",
    )(h2, w_ffn_in, w_ffn_in, conv_w, prev)


def _rope_tables(pos):
    half = HEAD_DIM // 2
    inv_freq = 1.0 / (ROPE_THETA ** (jnp.arange(half, dtype=F32) * 2.0 / HEAD_DIM))
    ang = pos.astype(F32)[:, None] * inv_freq[None, :]
    cos, sin = jnp.cos(ang), jnp.sin(ang)
    reps = V7X_LANES // HEAD_DIM
    cos_t = jnp.concatenate([cos, cos] * reps, axis=1)
    sin_t = jnp.concatenate([-sin, sin] * reps, axis=1)
    return cos_t, sin_t


def _pad_history(state, rows):
    b, k1, c = state.shape
    return jnp.concatenate([jnp.zeros((b, rows - k1, c), state.dtype), state], axis=1)


def _layer(x, h, layer, lam_init, final, group, weights):
    (w_in, conv_a_w, conv_a_b, ln_a_g, ln_a_b, w_a_out, lam_q1, lam_k1, lam_q2, lam_k2,
     subln_g, w_b_out, conv_c_w, w_c_out, w_o, norm_ffn, w_ffn_in, conv_ffn_w, w_ffn_down,
     g_next) = weights
    batch, seq, tm = group["batch"], group["seq"], group["tm"]
    m, d = x.shape
    d_a = conv_a_w.shape[-1]
    d_c = conv_c_w.shape[-1]
    d_ff = conv_ffn_w.shape[-1]
    qk_w = w_b_out.shape[1]
    taps_a, taps_c = conv_a_w.shape[1], conv_c_w.shape[1]
    offs = {}
    off = 0
    for name, width in (("a_val", d_a), ("a_gate", d_a), ("q", qk_w), ("k", qk_w), ("v", qk_w),
                        ("c_b", d_c), ("c_c", d_c), ("c_x", d_c),
                        ("g_a", d), ("g_b", d), ("g_c", d)):
        offs[name] = off
        off += width
    assert off == w_in.shape[-1]
    is_sample = group["cache"] is not None
    scale = HEAD_DIM ** -0.5
    lam_params = tuple(a[layer].reshape(1, HEAD_DIM) for a in (lam_q1, lam_k1, lam_q2, lam_k2))

    u_a = _proj_glu(h, w_in, layer, offs["a_val"], offs["a_gate"], d_a, tm)
    u_a3 = u_a.reshape(batch, seq, d_a)
    hist_a = -(-(taps_a - 1) // V7X_SUBLANES) * V7X_SUBLANES
    if is_sample:
        prev_a = _pad_history(group["state_a"][layer], hist_a)
        new_a = jnp.concatenate([group["state_a"][layer], u_a3], axis=1)[:, -(taps_a - 1):]
    else:
        prev_a = jnp.zeros((batch, hist_a, d_a), F32)
        new_a = u_a3[:, seq - (taps_a - 1):]
    t_a = _conv_ln_silu(u_a3, prev_a, conv_a_w[layer], conv_a_b[layer], ln_a_g[layer],
                        ln_a_b[layer], group["tt"]).reshape(m, d_a)

    cos, sin = group["rope"]
    n_heads = qk_w // HEAD_W
    proj = functools.partial(_proj_heads, h, w_in, layer, width=qk_w, cos=cos, sin=sin, tm=tm)
    if is_sample:
        (q_f,) = proj(off=offs["q"], rope=True, scale=scale, f32_layout="rows")
        (k_f,) = proj(off=offs["k"], rope=True, f32_layout="heads")
        (v_f,) = proj(off=offs["v"], rope=False, f32_layout="heads")
        cache_k, cache_v, page_table = group["cache"]
        t_b = _attn_sample(q_f.reshape(batch, seq, qk_w),
                           k_f.reshape(batch, seq * n_heads, HEAD_W),
                           v_f.reshape(batch, seq * n_heads, HEAD_W),
                           cache_k, cache_v, page_table, layer,
                           lam_params, subln_g[layer], lam_init).reshape(m, qk_w)
    else:
        tb = _tile(seq, PROMPT_ATTN_BLOCK)
        (q_b,) = proj(off=offs["q"], rope=True, scale=scale, out_bf16=True)
        k_f, k_b = proj(off=offs["k"], rope=True, f32_layout="heads", out_bf16=True)
        v_f, v_t = proj(off=offs["v"], rope=False, f32_layout="heads", transposed_tk=tb, seq=seq)
        t_b = _attn_prompt(q_b, k_b, v_t, lam_params, subln_g[layer], lam_init, batch, seq, tb)

    c_b, cc = _proj_cbranch(h, w_in, layer, offs["c_b"], offs["c_c"], offs["c_x"], d_c, tm)
    cc3 = cc.reshape(batch, seq, d_c)
    if is_sample:
        prev_c = _pad_history(group["state_c"][layer], V7X_SUBLANES)
        new_c = jnp.concatenate([group["state_c"][layer], cc3], axis=1)[:, -(taps_c - 1):]
    else:
        prev_c = jnp.zeros((batch, V7X_SUBLANES, d_c), F32)
        new_c = cc3[:, seq - (taps_c - 1):]
    t_c = _conv_mul(cc3, prev_c, conv_c_w[layer], c_b.reshape(batch, seq, d_c),
                    group["tt"]).reshape(m, d_c)

    merged = _merge(t_a, t_b, t_c, h, w_a_out, w_b_out, w_c_out, w_in, layer,
                    (offs["g_a"], offs["g_b"], offs["g_c"]), tm)
    x_mid, h2 = _residual_matmul_norm(x, merged, w_o, layer, norm_ffn[layer], tm, False,
                                      "wo_residual_norm")

    taps_f = conv_ffn_w.shape[1]
    if is_sample:
        prev_f = _pad_history(group["state_f"][layer], V7X_SUBLANES)
        act, g_up = _ffn_up(h2, w_ffn_in, conv_ffn_w, layer, prev_f, seq, tm)
        new_f = jnp.concatenate([group["state_f"][layer], g_up.reshape(batch, seq, d_ff)],
                                axis=1)[:, -(taps_f - 1):]
    else:
        act, g_tail = _ffn_up(h2, w_ffn_in, conv_ffn_w, layer, None, seq, tm)
        blocks_per_seq = seq // tm
        g_tail = g_tail.reshape(batch, blocks_per_seq, V7X_SUBLANES, d_ff)
        new_f = g_tail[:, blocks_per_seq - 1, V7X_SUBLANES - (taps_f - 1):]
    x_new, h_next = _residual_matmul_norm(x_mid, act, w_ffn_down, layer, g_next, tm, final,
                                          "ffn_down_residual_norm")
    k_out = k_f.reshape(batch, seq, n_heads, HEAD_W)
    v_out = v_f.reshape(batch, seq, n_heads, HEAD_W)
    return x_new, h_next, k_out, v_out, new_a, new_c, new_f


def kernel(x_prompt, x_sample, cache_k, cache_v, state_conv_a, state_conv_c, state_conv_ffn,
           page_table, norm_mix, w_in, conv_a_w, conv_a_b, ln_a_g, ln_a_b, w_a_out, lam_q1,
           lam_k1, lam_q2, lam_k2, subln_g, w_b_out, conv_c_w, w_c_out, w_o, norm_ffn, w_ffn_in,
           conv_ffn_w, w_ffn_down, norm_final):
    bp, t_p, d = x_prompt.shape
    bs, t_s, _ = x_sample.shape
    depth = w_in.shape[0]
    past = page_table.shape[1] * PAGE_SIZE

    tm_p = _tile(t_p, 1024)
    cos_p, sin_p = _rope_tables(jnp.arange(t_p, dtype=jnp.int32))
    cos_s, sin_s = _rope_tables(past + jnp.arange(t_s, dtype=jnp.int32))
    groups = [
        dict(batch=bp, seq=t_p, tm=tm_p, tt=_tile(t_p, 256),
             rope=(cos_p, sin_p), cache=None),
        dict(batch=bs, seq=t_s, tm=bs * t_s, tt=t_s,
             rope=(jnp.tile(cos_s, (bs, 1)), jnp.tile(sin_s, (bs, 1))),
             cache=(cache_k, cache_v, page_table),
             state_a=state_conv_a, state_c=state_conv_c, state_f=state_conv_ffn),
    ]
    w_o, w_ffn_in, w_ffn_down = (w.astype(BF16) for w in (w_o, w_ffn_in, w_ffn_down))
    xs = [x_prompt.reshape(bp * t_p, d), x_sample.reshape(bs * t_s, d)]
    hs = [_rmsnorm_bf16(x, norm_mix[0]) for x in xs]
    collected = [[], []]
    for layer in range(depth):
        lam_init = 0.8 - 0.6 * math.exp(-0.3 * layer)
        final = layer == depth - 1
        g_next = norm_final if final else norm_mix[layer + 1]
        weights = (w_in, conv_a_w, conv_a_b, ln_a_g, ln_a_b, w_a_out, lam_q1, lam_k1, lam_q2,
                   lam_k2, subln_g, w_b_out, conv_c_w, w_c_out, w_o, norm_ffn, w_ffn_in,
                   conv_ffn_w, w_ffn_down, g_next)
        for gi, group in enumerate(groups):
            x_new, h_next, k_out, v_out, new_a, new_c, new_f = _layer(
                xs[gi], hs[gi], layer, lam_init, final, group, weights)
            xs[gi], hs[gi] = x_new, h_next
            collected[gi].append((k_out, v_out, new_a, new_c, new_f))
    y_prompt = xs[0].reshape(bp, t_p, d)
    y_sample = xs[1].reshape(bs, t_s, d)
    stack = lambda gi, idx: jnp.stack([c[idx] for c in collected[gi]])
    return (y_prompt, y_sample,
            stack(0, 0), stack(0, 1), stack(0, 2), stack(0, 3), stack(0, 4),
            stack(1, 0), stack(1, 1), stack(1, 2), stack(1, 3), stack(1, 4))
```

```python
import functools
import math

import jax
import jax.numpy as jnp
from jax import lax
from jax.experimental import pallas as pl
from jax.experimental.pallas import tpu as pltpu

BF16 = jnp.bfloat16
F32 = jnp.float32

HEAD_DIM = 64
PAGE_SIZE = 128
ROPE_THETA = 10000.0
RMS_EPS = 1e-6
LN_EPS = 1e-5
SUBLN_EPS = 1e-5
NEG_INF = -1e30

V7X_LANES = 128
V7X_SUBLANES = 8
V7X_VMEM_LIMIT_BYTES = 60 * 1024 * 1024

HEAD_W = 2 * HEAD_DIM


def _tile(dim, pref):
    if dim <= pref:
        return dim
    t = pref
    while t >= V7X_SUBLANES:
        if dim % t == 0 and t % V7X_SUBLANES == 0:
            return t
        t -= V7X_SUBLANES
    return dim


def _tile_count(total, pref):
    return max(c for c in range(1, min(total, pref) + 1) if total % c == 0)


SAMPLE_PAGES_PER_STEP = 8
PROMPT_ATTN_BLOCK = 512
PROMPT_ATTN_HEADS_PER_STEP = 2
FFN_UP_ROWS = 1024
LOG2_E = 1.4426950408889634


def _params(sem):
    return pltpu.CompilerParams(dimension_semantics=sem,
                                vmem_limit_bytes=V7X_VMEM_LIMIT_BYTES)


def _dot(a, b):
    return jnp.dot(a, b, preferred_element_type=F32)


def _dot_nt(a, b):
    return lax.dot_general(a, b, (((1,), (1,)), ((), ())), preferred_element_type=F32)


def _rms(x, g, eps):
    return x * lax.rsqrt(jnp.mean(x * x, axis=-1, keepdims=True) + eps) * g


def _rmsnorm_kernel(x_ref, g_ref, o_ref):
    o_ref[...] = _rms(x_ref[...], g_ref[...], RMS_EPS).astype(o_ref.dtype)


def _rmsnorm_bf16(x, g):
    m, d = x.shape
    tm = _tile(m, 512)
    return pl.pallas_call(
        _rmsnorm_kernel,
        grid=(m // tm,),
        in_specs=[pl.BlockSpec((tm, d), lambda i: (i, 0)),
                  pl.BlockSpec((1, d), lambda i: (0, 0))],
        out_specs=pl.BlockSpec((tm, d), lambda i: (i, 0)),
        out_shape=jax.ShapeDtypeStruct((m, d), BF16),
        compiler_params=_params(("parallel",)),
        name="rmsnorm_in",
    )(x, g.reshape(1, d))


def _w_spec(d, tn, layer, col_off):
    assert col_off % tn == 0
    base = col_off // tn
    return pl.BlockSpec((None, d, tn), lambda i, j: (layer, 0, base + j))


def _glu_kernel(h_ref, wv_ref, wg_ref, o_ref):
    h = h_ref[...]
    val = _dot(h, wv_ref[...].astype(BF16))
    gate = _dot(h, wg_ref[...].astype(BF16))
    o_ref[...] = val * jax.nn.sigmoid(gate)


def _proj_glu(h, w_in, layer, off_val, off_gate, width, tm):
    m, d = h.shape
    tn = _tile(width, 512)
    return pl.pallas_call(
        _glu_kernel,
        grid=(m // tm, width // tn),
        in_specs=[pl.BlockSpec((tm, d), lambda i, j: (i, 0)),
                  _w_spec(d, tn, layer, off_val),
                  _w_spec(d, tn, layer, off_gate)],
        out_specs=pl.BlockSpec((tm, tn), lambda i, j: (i, j)),
        out_shape=jax.ShapeDtypeStruct((m, width), F32),
        compiler_params=_params(("parallel", "arbitrary")),
        name="proj_glu",
    )(h, w_in, w_in)


def _rope_tile(x, cos, sin_signed):
    lane = lax.broadcasted_iota(jnp.int32, x.shape, 1)
    first_half = (lane % HEAD_DIM) < (HEAD_DIM // 2)
    partner = jnp.where(first_half,
                        pltpu.roll(x, V7X_LANES - HEAD_DIM // 2, 1),
                        pltpu.roll(x, HEAD_DIM // 2, 1))
    return x * cos + partner * sin_signed


def _heads_kernel(h_ref, w_ref, cos_ref, sin_ref, *o_refs, rope, scale, f32_layout, out_bf16,
                  transposed_tk):
    acc = _dot(h_ref[...], w_ref[...].astype(BF16))
    tm, width = acc.shape
    n_heads = width // HEAD_W
    for head in range(n_heads):
        sl = slice(head * HEAD_W, (head + 1) * HEAD_W)
        y = acc[:, sl]
        if rope:
            y = _rope_tile(y, cos_ref[...], sin_ref[...])
        if scale != 1.0:
            y = y * scale
        k = 0
        if f32_layout == "rows":
            o_refs[k][:, sl] = y
            k += 1
        elif f32_layout == "heads":
            o_refs[k][pl.ds(head, tm, stride=n_heads), :] = y
            k += 1
        if out_bf16:
            o_refs[k][:, sl] = y.astype(BF16)
            k += 1
        if transposed_tk:
            for c in range(tm // transposed_tk):
                rows = slice(c * transposed_tk, (c + 1) * transposed_tk)
                o_refs[k][head, c] = y[rows, :].T.astype(BF16)


def _proj_heads(h, w_in, layer, off, width, cos, sin, tm, *, rope, scale=1.0, f32_layout=None,
                out_bf16=False, transposed_tk=0, seq=None):
    m, d = h.shape
    n_heads = width // HEAD_W
    n_pos_blocks = cos.shape[0] // tm
    tab_spec = pl.BlockSpec((tm, V7X_LANES), lambda i: (i % n_pos_blocks, 0))
    out_specs, out_shape = [], []
    if f32_layout == "rows":
        out_specs.append(pl.BlockSpec((tm, width), lambda i: (i, 0)))
        out_shape.append(jax.ShapeDtypeStruct((m, width), F32))
    elif f32_layout == "heads":
        out_specs.append(pl.BlockSpec((tm * n_heads, HEAD_W), lambda i: (i, 0)))
        out_shape.append(jax.ShapeDtypeStruct((m * n_heads, HEAD_W), F32))
    if out_bf16:
        out_specs.append(pl.BlockSpec((tm, width), lambda i: (i, 0)))
        out_shape.append(jax.ShapeDtypeStruct((m, width), BF16))
    if transposed_tk:
        assert tm % transposed_tk == 0 and seq % tm == 0
        blocks_per_seq = seq // tm
        sub = tm // transposed_tk
        out_specs.append(pl.BlockSpec(
            (None, n_heads, sub, HEAD_W, transposed_tk),
            lambda i: (i // blocks_per_seq, 0, i % blocks_per_seq, 0, 0)))
        out_shape.append(jax.ShapeDtypeStruct(
            (m // seq, n_heads, seq // transposed_tk, HEAD_W, transposed_tk), BF16))
    assert off % width == 0
    return pl.pallas_call(
        functools.partial(_heads_kernel, rope=rope, scale=scale, f32_layout=f32_layout,
                          out_bf16=out_bf16, transposed_tk=transposed_tk),
        grid=(m // tm,),
        in_specs=[pl.BlockSpec((tm, d), lambda i: (i, 0)),
                  pl.BlockSpec((None, d, width), lambda i: (layer, 0, off // width)),
                  tab_spec, tab_spec],
        out_specs=out_specs,
        out_shape=out_shape,
        compiler_params=_params(("parallel",)),
        name="proj_heads",
    )(h, w_in, cos, sin)


def _cbranch_kernel(h_ref, wb_ref, wc_ref, wx_ref, cb_ref, cc_ref):
    h = h_ref[...]
    cb_ref[...] = _dot(h, wb_ref[...].astype(BF16)).astype(cb_ref.dtype)
    c_c = _dot(h, wc_ref[...].astype(BF16))
    c_x = _dot(h, wx_ref[...].astype(BF16))
    cc_ref[...] = c_c * c_x


def _proj_cbranch(h, w_in, layer, off_b, off_c, off_x, width, tm):
    m, d = h.shape
    tn = _tile(width, 512)
    return pl.pallas_call(
        _cbranch_kernel,
        grid=(m // tm, width // tn),
        in_specs=[pl.BlockSpec((tm, d), lambda i, j: (i, 0)),
                  _w_spec(d, tn, layer, off_b),
                  _w_spec(d, tn, layer, off_c),
                  _w_spec(d, tn, layer, off_x)],
        out_specs=[pl.BlockSpec((tm, tn), lambda i, j: (i, j)),
                   pl.BlockSpec((tm, tn), lambda i, j: (i, j))],
        out_shape=[jax.ShapeDtypeStruct((m, width), BF16),
                   jax.ShapeDtypeStruct((m, width), F32)],
        compiler_params=_params(("parallel", "arbitrary")),
        name="proj_cbranch",
    )(h, w_in, w_in, w_in)


def _dwconv_fill(u_ref, prev_ref, buf, hist_rows, tt):
    ti = pl.program_id(1)

    @pl.when(ti == 0)
    def _():
        buf[0:hist_rows, :] = prev_ref[...]

    @pl.when(ti > 0)
    def _():
        buf[0:hist_rows, :] = buf[tt:tt + hist_rows, :]

    buf[hist_rows:hist_rows + tt, :] = u_ref[...]


def _dwconv_rows(buf, w_ref, taps, hist_rows, r0, rows, c0, cols):
    base = hist_rows - (taps - 1)
    acc = None
    for k in range(taps):
        term = buf[r0 + base + k:r0 + base + k + rows, c0:c0 + cols] * w_ref[k:k + 1, c0:c0 + cols]
        acc = term if acc is None else acc + term
    return acc


def _conv_ln_silu_kernel(u_ref, prev_ref, w_ref, b_ref, g_ref, beta_ref, o_ref, buf, shifted,
                         ybuf, *, taps, hist_rows, tt, row_chunk, col_chunk):
    _dwconv_fill(u_ref, prev_ref, buf, hist_rows, tt)
    c = u_ref.shape[-1]
    base = hist_rows - (taps - 1)
    span = shifted.shape[1]
    for res in range(1, V7X_SUBLANES):
        shifted[res - 1] = buf[res:res + span, :]
    for r0 in range(0, tt, row_chunk):
        for c0 in range(0, c, col_chunk):
            acc = None
            for k in range(taps):
                res, tile_off = (base + k) % V7X_SUBLANES, (base + k) // V7X_SUBLANES * V7X_SUBLANES
                rows = slice(r0 + tile_off, r0 + tile_off + row_chunk)
                cols = slice(c0, c0 + col_chunk)
                window = buf[rows, cols] if res == 0 else shifted[res - 1, rows, cols]
                term = window * w_ref[k:k + 1, cols]
                acc = term if acc is None else acc + term
            ybuf[r0:r0 + row_chunk, c0:c0 + col_chunk] = acc
    y = ybuf[...] + b_ref[...]
    mu = jnp.mean(y, axis=-1, keepdims=True)
    yc = y - mu
    var = jnp.mean(yc * yc, axis=-1, keepdims=True)
    z = yc * lax.rsqrt(var + LN_EPS) * g_ref[...] + beta_ref[...]
    o_ref[...] = (z * jax.nn.sigmoid(z)).astype(o_ref.dtype)


def _conv_ln_silu(u, prev, w, bias, ln_g, ln_b, tt):
    b, t, c = u.shape
    taps = w.shape[0]
    hist_rows = prev.shape[1]
    row_chunk = min(tt, 32)
    col_chunk = min(c, 512)
    vec = lambda a: a.reshape(1, c)
    return pl.pallas_call(
        functools.partial(_conv_ln_silu_kernel, taps=taps, hist_rows=hist_rows, tt=tt,
                          row_chunk=row_chunk, col_chunk=col_chunk),
        grid=(b, t // tt),
        in_specs=[pl.BlockSpec((None, tt, c), lambda bi, ti: (bi, ti, 0)),
                  pl.BlockSpec((None, hist_rows, c), lambda bi, ti: (bi, 0, 0)),
                  pl.BlockSpec((taps, c), lambda bi, ti: (0, 0)),
                  pl.BlockSpec((1, c), lambda bi, ti: (0, 0)),
                  pl.BlockSpec((1, c), lambda bi, ti: (0, 0)),
                  pl.BlockSpec((1, c), lambda bi, ti: (0, 0))],
        out_specs=pl.BlockSpec((None, tt, c), lambda bi, ti: (bi, ti, 0)),
        out_shape=jax.ShapeDtypeStruct((b, t, c), BF16),
        scratch_shapes=[pltpu.VMEM((hist_rows + tt, c), F32),
                        pltpu.VMEM((V7X_SUBLANES - 1, tt + hist_rows - V7X_SUBLANES, c), F32),
                        pltpu.VMEM((tt, c), F32)],
        compiler_params=_params(("parallel", "arbitrary")),
        name="conv_a",
    )(u, prev, w, vec(bias), vec(ln_g), vec(ln_b))


def _conv_mul_kernel(u_ref, prev_ref, w_ref, cb_ref, o_ref, buf, *, taps, hist_rows, tt):
    _dwconv_fill(u_ref, prev_ref, buf, hist_rows, tt)
    c = u_ref.shape[-1]
    conv = _dwconv_rows(buf, w_ref, taps, hist_rows, 0, tt, 0, c)
    o_ref[...] = (cb_ref[...].astype(F32) * conv).astype(o_ref.dtype)


def _conv_mul(u, prev, w, cb, tt):
    b, t, c = u.shape
    taps = w.shape[0]
    hist_rows = prev.shape[1]
    return pl.pallas_call(
        functools.partial(_conv_mul_kernel, taps=taps, hist_rows=hist_rows, tt=tt),
        grid=(b, t // tt),
        in_specs=[pl.BlockSpec((None, tt, c), lambda bi, ti: (bi, ti, 0)),
                  pl.BlockSpec((None, hist_rows, c), lambda bi, ti: (bi, 0, 0)),
                  pl.BlockSpec((taps, c), lambda bi, ti: (0, 0)),
                  pl.BlockSpec((None, tt, c), lambda bi, ti: (bi, ti, 0))],
        out_specs=pl.BlockSpec((None, tt, c), lambda bi, ti: (bi, ti, 0)),
        out_shape=jax.ShapeDtypeStruct((b, t, c), BF16),
        scratch_shapes=[pltpu.VMEM((hist_rows + tt, c), F32)],
        compiler_params=_params(("parallel", "arbitrary")),
        name="conv_c",
    )(u, prev, w, cb)


def _lambda(lq1_ref, lk1_ref, lq2_ref, lk2_ref, lam_init):
    s1 = jnp.sum(lq1_ref[...] * lk1_ref[...], axis=-1, keepdims=True)
    s2 = jnp.sum(lq2_ref[...] * lk2_ref[...], axis=-1, keepdims=True)
    return jnp.exp(s1) - jnp.exp(s2) + lam_init


def _subln(o, g, lam_init):
    return _rms(o, g, SUBLN_EPS) * (1.0 - lam_init)


def _attn_prompt_kernel(q_ref, k_ref, vt_ref, lq1_ref, lk1_ref, lq2_ref, lk2_ref, gs_ref,
                        o_ref, s_ref, m_ref, l_ref, acc_ref, *, tb, lam_init, heads):
    qi = pl.program_id(2)
    lane = lax.broadcasted_iota(jnp.int32, (tb, HEAD_W), 1)
    q_maps = []
    for hh in range(heads):
        q = q_ref[:, hh * HEAD_W:(hh + 1) * HEAD_W]
        zero = jnp.zeros_like(q)
        q_maps.append((jnp.where(lane < HEAD_DIM, q, zero), jnp.where(lane >= HEAD_DIM, q, zero)))
    m_ref[...] = jnp.full(m_ref.shape, NEG_INF, F32)
    l_ref[...] = jnp.zeros(l_ref.shape, F32)
    acc_ref[...] = jnp.zeros(acc_ref.shape, F32)

    def scores(kj, slot):
        rows = pl.ds(pl.multiple_of(kj * tb, tb), tb)
        for hh in range(heads):
            k = k_ref[rows, hh * HEAD_W:(hh + 1) * HEAD_W]
            for mp in range(2):
                s_ref[slot, hh, mp] = _dot_nt(k, q_maps[hh][mp])

    def consume(kj, slot, masked):
        for hh in range(heads):
            vt = vt_ref[hh, kj]
            for mp in range(2):
                s = s_ref[slot, hh, mp]
                if masked:
                    key = lax.broadcasted_iota(jnp.int32, s.shape, 0)
                    qry = lax.broadcasted_iota(jnp.int32, s.shape, 1)
                    s = jnp.where(key <= qry, s, NEG_INF)
                m_prev = m_ref[hh, mp]
                m_new = jnp.maximum(m_prev, jnp.max(s, axis=0, keepdims=True))
                alpha = jnp.exp2(m_prev - m_new)
                p = jnp.exp2(s - m_new)
                l_ref[hh, mp] = alpha * l_ref[hh, mp] + jnp.sum(p, axis=0, keepdims=True)
                acc_ref[hh, mp] = alpha * acc_ref[hh, mp] + _dot(vt, p.astype(BF16))
                m_ref[hh, mp] = m_new

    scores(0, 0)

    def block_pair(t, carry):
        scores(2 * t + 1, 1)
        consume(2 * t, 0, False)
        scores(2 * t + 2, 0)
        consume(2 * t + 1, 1, False)
        return carry

    lax.fori_loop(0, qi // 2, block_pair, 0)

    @pl.when(qi % 2 == 0)
    def _():
        consume(qi, 0, True)

    @pl.when(qi % 2 == 1)
    def _():
        scores(qi, 1)
        consume(qi - 1, 0, False)
        consume(qi, 1, True)

    lam = _lambda(lq1_ref, lk1_ref, lq2_ref, lk2_ref, lam_init)
    for hh in range(heads):
        diff = acc_ref[hh, 0] / l_ref[hh, 0] - lam * (acc_ref[hh, 1] / l_ref[hh, 1])
        ms = jnp.mean(diff * diff, axis=0, keepdims=True)
        normed = diff * lax.rsqrt(ms + SUBLN_EPS) * gs_ref[...] * (1.0 - lam_init)
        o_ref[:, hh * HEAD_W:(hh + 1) * HEAD_W] = normed.T.astype(o_ref.dtype)


def _attn_prompt(qb, kb, vt, lam_params, g_sub, lam_init, batch, seq, tb):
    m, w = qb.shape
    n_heads = w // HEAD_W
    heads = _tile_count(n_heads, PROMPT_ATTN_HEADS_PER_STEP)
    hw = heads * HEAD_W
    nq = seq // tb
    small = lambda n: pl.BlockSpec((1, n), lambda b, h, qi: (0, 0))
    return pl.pallas_call(
        functools.partial(_attn_prompt_kernel, tb=tb, lam_init=lam_init, heads=heads),
        grid=(batch, n_heads // heads, nq),
        in_specs=[pl.BlockSpec((tb, hw), lambda b, h, qi: (b * nq + qi, h)),
                  pl.BlockSpec((seq, hw), lambda b, h, qi: (b, h)),
                  pl.BlockSpec((None, heads, nq, HEAD_W, tb), lambda b, h, qi: (b, h, 0, 0, 0)),
                  small(HEAD_DIM), small(HEAD_DIM), small(HEAD_DIM), small(HEAD_DIM),
                  pl.BlockSpec((HEAD_W, 1), lambda b, h, qi: (0, 0))],
        out_specs=pl.BlockSpec((tb, hw), lambda b, h, qi: (b * nq + qi, h)),
        out_shape=jax.ShapeDtypeStruct((m, w), BF16),
        scratch_shapes=[pltpu.VMEM((2, heads, 2, tb, tb), F32),
                        pltpu.VMEM((heads, 2, 1, tb), F32),
                        pltpu.VMEM((heads, 2, 1, tb), F32),
                        pltpu.VMEM((heads, 2, HEAD_W, tb), F32)],
        compiler_params=_params(("parallel", "parallel", "arbitrary")),
        name="attn_prompt",
    )(qb, kb, vt, *lam_params, g_sub.reshape(HEAD_W, 1))


def _attn_sample_kernel(pt_ref, q_ref, kn_ref, vn_ref, *refs, t_new, n_heads, lam_init, n_pages):
    page_refs = refs[:2 * n_pages]
    (lq1_ref, lk1_ref, lq2_ref, lk2_ref, gs_ref, o_ref,
     qbd_ref, bias_ref, m_ref, l_ref, acc_ref) = refs[2 * n_pages:]
    del pt_ref
    p = pl.program_id(1)
    rows_per_head = 2 * t_new

    @pl.when(p == 0)
    def _():
        q = q_ref[...]
        lane = lax.broadcasted_iota(jnp.int32, (t_new, HEAD_W), 1)
        for h in range(n_heads):
            qh = q[:, h * HEAD_W:(h + 1) * HEAD_W]
            r0 = h * rows_per_head
            qbd_ref[r0:r0 + t_new, :] = jnp.where(lane < HEAD_DIM, qh, 0.0)
            qbd_ref[r0 + t_new:r0 + rows_per_head, :] = jnp.where(lane >= HEAD_DIM, qh, 0.0)
        m_ref[...] = jnp.full(m_ref.shape, NEG_INF, F32)
        l_ref[...] = jnp.zeros(l_ref.shape, F32)
        acc_ref[...] = jnp.zeros(acc_ref.shape, F32)

        row = lax.broadcasted_iota(jnp.int32, bias_ref.shape, 0)
        col = lax.broadcasted_iota(jnp.int32, bias_ref.shape, 1)
        bias_ref[...] = jnp.where(col % n_heads == row // rows_per_head, 0.0, NEG_INF)

    def update(scores, values):
        m_prev = m_ref[...]
        m_new = m_prev
        for s in scores:
            m_new = jnp.maximum(m_new, jnp.max(s, axis=-1, keepdims=True))
        alpha = jnp.exp2(m_prev - m_new)
        l_new = alpha * l_ref[...]
        acc = alpha * acc_ref[...]
        for s, v in zip(scores, values):
            pr = jnp.exp2(s - m_new)
            l_new = l_new + jnp.sum(pr, axis=-1, keepdims=True)
            acc = acc + _dot(pr.astype(BF16), v.astype(BF16))
        l_ref[...] = l_new
        acc_ref[...] = acc
        m_ref[...] = m_new

    qbd = qbd_ref[...].astype(BF16)
    page_rows = PAGE_SIZE * n_heads
    bias = bias_ref[...]
    scores = [_dot_nt(qbd, k_ref[...].reshape(page_rows, HEAD_W).astype(BF16)) + bias
              for k_ref in page_refs[:n_pages]]
    values = [v_ref[...].reshape(page_rows, HEAD_W) for v_ref in page_refs[n_pages:]]
    update(scores, values)

    @pl.when(p == pl.num_programs(1) - 1)
    def _():
        s = _dot_nt(qbd, kn_ref[...].astype(BF16))
        row = lax.broadcasted_iota(jnp.int32, s.shape, 0)
        col = lax.broadcasted_iota(jnp.int32, s.shape, 1)
        valid = jnp.logical_and(col % n_heads == row // rows_per_head,
                                col // n_heads <= row % t_new)
        update([jnp.where(valid, s, NEG_INF)], [vn_ref[...]])
        lam = _lambda(lq1_ref, lk1_ref, lq2_ref, lk2_ref, lam_init)
        o = acc_ref[...] / l_ref[...]
        for h in range(n_heads):
            r0 = h * rows_per_head
            diff = o[r0:r0 + t_new, :] - lam * o[r0 + t_new:r0 + rows_per_head, :]
            o_ref[:, h * HEAD_W:(h + 1) * HEAD_W] = _subln(
                diff, gs_ref[...], lam_init).astype(o_ref.dtype)


def _attn_sample(q, k_new, v_new, cache_k, cache_v, page_table, layer, lam_params, g_sub,
                 lam_init):
    b, t_new, w = q.shape
    n_heads = w // HEAD_W
    total_pages = page_table.shape[1]
    n_pages = _tile_count(total_pages, SAMPLE_PAGES_PER_STEP)
    assert t_new % V7X_SUBLANES == 0
    rows = 2 * n_heads * t_new

    q_spec = pl.BlockSpec((None, t_new, w), lambda bi, p, pt: (bi, 0, 0))
    new_spec = pl.BlockSpec((None, t_new * n_heads, HEAD_W), lambda bi, p, pt: (bi, 0, 0))
    page_shape = (None, None, PAGE_SIZE, n_heads, HEAD_W)
    page_specs = [
        pl.BlockSpec(page_shape, lambda bi, p, pt, c=c: (layer, pt[bi, n_pages * p + c], 0, 0, 0))
        for c in range(n_pages)]
    small = lambda n: pl.BlockSpec((1, n), lambda bi, p, pt: (0, 0))
    grid_spec = pltpu.PrefetchScalarGridSpec(
        num_scalar_prefetch=1,
        grid=(b, total_pages // n_pages),
        in_specs=[q_spec, new_spec, new_spec] + page_specs + page_specs
                 + [small(HEAD_DIM), small(HEAD_DIM), small(HEAD_DIM), small(HEAD_DIM),
                    small(HEAD_W)],
        out_specs=pl.BlockSpec((None, t_new, w), lambda bi, p, pt: (bi, 0, 0)),
        scratch_shapes=[pltpu.VMEM((rows, HEAD_W), F32),
                        pltpu.VMEM((rows, PAGE_SIZE * n_heads), F32),
                        pltpu.VMEM((rows, 1), F32),
                        pltpu.VMEM((rows, 1), F32),
                        pltpu.VMEM((rows, HEAD_W), F32)],
    )
    return pl.pallas_call(
        functools.partial(_attn_sample_kernel, t_new=t_new, n_heads=n_heads, lam_init=lam_init,
                          n_pages=n_pages),
        grid_spec=grid_spec,
        out_shape=jax.ShapeDtypeStruct((b, t_new, w), BF16),
        compiler_params=_params(("parallel", "arbitrary")),
        name="attn_sample",
    )(page_table, q, k_new, v_new, *([cache_k] * n_pages), *([cache_v] * n_pages), *lam_params,
      g_sub.reshape(1, HEAD_W))


def _merge_kernel(ta_ref, tb_ref, tc_ref, h_ref, wa_ref, wb_ref, wc_ref,
                  wga_ref, wgb_ref, wgc_ref, o_ref):
    h = h_ref[...]

    def branch(t_ref, w_ref, wg_ref):
        y = _dot(t_ref[...], w_ref[...].astype(BF16))
        g = _dot(h, wg_ref[...].astype(BF16))
        return jax.nn.sigmoid(g) * y

    merged = (branch(ta_ref, wa_ref, wga_ref) + branch(tb_ref, wb_ref, wgb_ref)
              + branch(tc_ref, wc_ref, wgc_ref))
    o_ref[...] = merged.astype(o_ref.dtype)


def _merge(ta, tb, tc, h, w_a_out, w_b_out, w_c_out, w_in, layer, gate_offs, tm):
    m, d = h.shape
    tn = _tile(d, 256)
    act = lambda a: pl.BlockSpec((tm, a.shape[1]), lambda i, j: (i, 0))
    wout = lambda a: pl.BlockSpec((None, a.shape[1], tn), lambda i, j: (layer, 0, j))
    return pl.pallas_call(
        _merge_kernel,
        grid=(m // tm, d // tn),
        in_specs=[act(ta), act(tb), act(tc), act(h),
                  wout(w_a_out), wout(w_b_out), wout(w_c_out),
                  _w_spec(d, tn, layer, gate_offs[0]),
                  _w_spec(d, tn, layer, gate_offs[1]),
                  _w_spec(d, tn, layer, gate_offs[2])],
        out_specs=pl.BlockSpec((tm, tn), lambda i, j: (i, j)),
        out_shape=jax.ShapeDtypeStruct((m, d), BF16),
        compiler_params=_params(("parallel", "arbitrary")),
        name="merge",
    )(ta, tb, tc, h, w_a_out, w_b_out, w_c_out, w_in, w_in, w_in)


def _residual_kernel(x_ref, lhs_ref, w_ref, g_ref, xo_ref, *h_refs, col_chunk, final):
    k = pl.program_id(1)

    @pl.when(k == 0)
    def _():
        xo_ref[...] = x_ref[...]

    lhs = lhs_ref[...]
    for c0 in range(0, xo_ref.shape[1], col_chunk):
        xo_ref[:, c0:c0 + col_chunk] += _dot(lhs, w_ref[:, c0:c0 + col_chunk].astype(BF16))

    @pl.when(k == pl.num_programs(1) - 1)
    def _():
        y = _rms(xo_ref[...], g_ref[...], RMS_EPS)
        if final:
            xo_ref[...] = y
        else:
            h_refs[0][...] = y.astype(h_refs[0].dtype)


def _residual_matmul_norm(x, lhs, w, layer, g_next, tm, final, name):
    m, d = x.shape
    kdim = lhs.shape[1]
    tk = _tile(kdim, 512)
    row_spec = pl.BlockSpec((tm, d), lambda i, k: (i, 0))
    out_specs = [row_spec] + ([] if final else [row_spec])
    out_shape = ([jax.ShapeDtypeStruct((m, d), F32)]
                 + ([] if final else [jax.ShapeDtypeStruct((m, d), BF16)]))
    outs = pl.pallas_call(
        functools.partial(_residual_kernel, col_chunk=_tile(d, 512), final=final),
        grid=(m // tm, kdim // tk),
        in_specs=[row_spec,
                  pl.BlockSpec((tm, tk), lambda i, k: (i, k)),
                  pl.BlockSpec((None, tk, d), lambda i, k: (layer, k, 0)),
                  pl.BlockSpec((1, d), lambda i, k: (0, 0))],
        out_specs=out_specs,
        out_shape=out_shape,
        compiler_params=_params(("parallel", "arbitrary")),
        name=name,
    )(x, lhs, w, g_next.reshape(1, d))
    return (outs[0], None) if final else tuple(outs)


def _ffn_up_kernel(h_ref, wg_ref, wv_ref, wc_ref, prev_ref, act_ref, gt_ref, buf_ref, hist_ref,
                   *, stream, seq_len, blocks_per_seq, sub):
    i = pl.program_id(0)
    j = pl.program_id(1)
    h = h_ref[...]
    tm = h.shape[0]
    fc = act_ref.shape[1]
    hist = V7X_SUBLANES
    nseq = tm // seq_len

    if stream:
        @pl.when(i % blocks_per_seq == 0)
        def _():
            buf_ref[:, 0:hist, :] = jnp.zeros((1, hist, fc), F32)

        @pl.when(i % blocks_per_seq != 0)
        def _():
            buf_ref[:, 0:hist, :] = hist_ref[j]
    else:
        buf_ref[:, 0:hist, :] = prev_ref[...]

    for c0 in range(0, fc, sub):
        cols = slice(c0, c0 + sub)
        gate = _dot(h, wg_ref[:, cols].astype(BF16))
        val = _dot(h, wv_ref[:, cols].astype(BF16))
        if stream:
            hist_ref[j, :, :, cols] = gate[tm - hist:tm, :].reshape(1, hist, sub)
            gt_ref[:, cols] = gate[tm - hist:tm, :]
        else:
            gt_ref[:, cols] = gate
        buf_ref[:, hist:hist + seq_len, cols] = gate.reshape(nseq, seq_len, sub)
        w = wc_ref[:, cols]
        conv = (buf_ref[:, hist - 2:hist - 2 + seq_len, cols] * w[0:1, :].reshape(1, 1, sub)
                + buf_ref[:, hist - 1:hist - 1 + seq_len, cols] * w[1:2, :].reshape(1, 1, sub)
                + buf_ref[:, hist:hist + seq_len, cols] * w[2:3, :].reshape(1, 1, sub))
        conv = conv.reshape(tm, sub)
        act_ref[:, cols] = (conv * jax.nn.sigmoid(conv) * val).astype(act_ref.dtype)


def _ffn_up(h2, w_ffn_in, conv_w, layer, prev, seq_len, tm):
    m, d = h2.shape
    dff = conv_w.shape[-1]
    fc = _tile(dff, 512)
    nj = dff // fc
    stream = prev is None
    if stream:
        assert seq_len % tm == 0
        blocks_per_seq = seq_len // tm
        rows_in_block = tm
        prev = jnp.zeros((1, V7X_SUBLANES, dff), F32)
        prev_spec = pl.BlockSpec((1, V7X_SUBLANES, fc), lambda i, j: (0, 0, j))
        gt_shape = jax.ShapeDtypeStruct((m // tm, V7X_SUBLANES, dff), F32)
        gt_spec = pl.BlockSpec((None, V7X_SUBLANES, fc), lambda i, j: (i, 0, j))
        nseq = 1
    else:
        assert tm % seq_len == 0 and m == tm
        blocks_per_seq = 1
        rows_in_block = seq_len
        nseq = tm // seq_len
        prev_spec = pl.BlockSpec((nseq, V7X_SUBLANES, fc), lambda i, j: (0, 0, j))
        gt_shape = jax.ShapeDtypeStruct((m, dff), F32)
        gt_spec = pl.BlockSpec((tm, fc), lambda i, j: (i, j))
    return pl.pallas_call(
        functools.partial(_ffn_up_kernel, stream=stream, seq_len=rows_in_block,
                          blocks_per_seq=blocks_per_seq,
                          sub=fc // 2 if fc % (4 * V7X_LANES) == 0 else fc),
        grid=(m // tm, nj),
        in_specs=[pl.BlockSpec((tm, d), lambda i, j: (i, 0)),
                  pl.BlockSpec((None, d, fc), lambda i, j: (layer, 0, j)),
                  pl.BlockSpec((None, d, fc), lambda i, j: (layer, 0, nj + j)),
                  pl.BlockSpec((None, conv_w.shape[1], fc), lambda i, j: (layer, 0, j)),
                  prev_spec],
        out_specs=[pl.BlockSpec((tm, fc), lambda i, j: (i, j)), gt_spec],
        out_shape=[jax.ShapeDtypeStruct((m, dff), BF16), gt_shape],
        scratch_shapes=[pltpu.VMEM((nseq, V7X_SUBLANES + rows_in_block, fc), F32),
                        pltpu.VMEM((nj, 1, V7X_SUBLANES, fc), F32)],
        compiler_params=_params(("arbitrary", "arbitrary")),
        name="ffn_up",
    )(h2, w_ffn_in, w_ffn_in, conv_w, prev)


def _rope_tables(pos):
    half = HEAD_DIM // 2
    inv_freq = 1.0 / (ROPE_THETA ** (jnp.arange(half, dtype=F32) * 2.0 / HEAD_DIM))
    ang = pos.astype(F32)[:, None] * inv_freq[None, :]
    cos, sin = jnp.cos(ang), jnp.sin(ang)
    reps = V7X_LANES // HEAD_DIM
    cos_t = jnp.concatenate([cos, cos] * reps, axis=1)
    sin_t = jnp.concatenate([-sin, sin] * reps, axis=1)
    return cos_t, sin_t


def _pad_history(state, rows):
    b, k1, c = state.shape
    return jnp.concatenate([jnp.zeros((b, rows - k1, c), state.dtype), state], axis=1)


def _layer(x, h, layer, lam_init, final, group, weights):
    (w_in, conv_a_w, conv_a_b, ln_a_g, ln_a_b, w_a_out, lam_q1, lam_k1, lam_q2, lam_k2,
     subln_g, w_b_out, conv_c_w, w_c_out, w_o, norm_ffn, w_ffn_in, conv_ffn_w, w_ffn_down,
     g_next) = weights
    batch, seq, tm = group["batch"], group["seq"], group["tm"]
    m, d = x.shape
    d_a = conv_a_w.shape[-1]
    d_c = conv_c_w.shape[-1]
    d_ff = conv_ffn_w.shape[-1]
    qk_w = w_b_out.shape[1]
    taps_a, taps_c = conv_a_w.shape[1], conv_c_w.shape[1]
    offs = {}
    off = 0
    for name, width in (("a_val", d_a), ("a_gate", d_a), ("q", qk_w), ("k", qk_w), ("v", qk_w),
                        ("c_b", d_c), ("c_c", d_c), ("c_x", d_c),
                        ("g_a", d), ("g_b", d), ("g_c", d)):
        offs[name] = off
        off += width
    assert off == w_in.shape[-1]
    is_sample = group["cache"] is not None
    scale = HEAD_DIM ** -0.5 * LOG2_E
    lam_params = tuple(a[layer].reshape(1, HEAD_DIM) for a in (lam_q1, lam_k1, lam_q2, lam_k2))

    u_a = _proj_glu(h, w_in, layer, offs["a_val"], offs["a_gate"], d_a, tm)
    u_a3 = u_a.reshape(batch, seq, d_a)
    hist_a = -(-(taps_a - 1) // V7X_SUBLANES) * V7X_SUBLANES
    if is_sample:
        prev_a = _pad_history(group["state_a"][layer], hist_a)
        new_a = jnp.concatenate([group["state_a"][layer], u_a3], axis=1)[:, -(taps_a - 1):]
    else:
        prev_a = jnp.zeros((batch, hist_a, d_a), F32)
        new_a = u_a3[:, seq - (taps_a - 1):]
    t_a = _conv_ln_silu(u_a3, prev_a, conv_a_w[layer], conv_a_b[layer], ln_a_g[layer],
                        ln_a_b[layer], group["tt"]).reshape(m, d_a)

    cos, sin = group["rope"]
    n_heads = qk_w // HEAD_W
    proj = functools.partial(_proj_heads, h, w_in, layer, width=qk_w, cos=cos, sin=sin, tm=tm)
    if is_sample:
        (q_f,) = proj(off=offs["q"], rope=True, scale=scale, f32_layout="rows")
        (k_f,) = proj(off=offs["k"], rope=True, f32_layout="heads")
        (v_f,) = proj(off=offs["v"], rope=False, f32_layout="heads")
        cache_k, cache_v, page_table = group["cache"]
        t_b = _attn_sample(q_f.reshape(batch, seq, qk_w),
                           k_f.reshape(batch, seq * n_heads, HEAD_W),
                           v_f.reshape(batch, seq * n_heads, HEAD_W),
                           cache_k, cache_v, page_table, layer,
                           lam_params, subln_g[layer], lam_init).reshape(m, qk_w)
    else:
        tb = _tile(seq, PROMPT_ATTN_BLOCK)
        (q_b,) = proj(off=offs["q"], rope=True, scale=scale, out_bf16=True)
        k_f, k_b = proj(off=offs["k"], rope=True, f32_layout="heads", out_bf16=True)
        v_f, v_t = proj(off=offs["v"], rope=False, f32_layout="heads", transposed_tk=tb, seq=seq)
        t_b = _attn_prompt(q_b, k_b, v_t, lam_params, subln_g[layer], lam_init, batch, seq, tb)

    c_b, cc = _proj_cbranch(h, w_in, layer, offs["c_b"], offs["c_c"], offs["c_x"], d_c, tm)
    cc3 = cc.reshape(batch, seq, d_c)
    if is_sample:
        prev_c = _pad_history(group["state_c"][layer], V7X_SUBLANES)
        new_c = jnp.concatenate([group["state_c"][layer], cc3], axis=1)[:, -(taps_c - 1):]
    else:
        prev_c = jnp.zeros((batch, V7X_SUBLANES, d_c), F32)
        new_c = cc3[:, seq - (taps_c - 1):]
    t_c = _conv_mul(cc3, prev_c, conv_c_w[layer], c_b.reshape(batch, seq, d_c),
                    group["tt"]).reshape(m, d_c)

    merged = _merge(t_a, t_b, t_c, h, w_a_out, w_b_out, w_c_out, w_in, layer,
                    (offs["g_a"], offs["g_b"], offs["g_c"]), tm)
    x_mid, h2 = _residual_matmul_norm(x, merged, w_o, layer, norm_ffn[layer], tm, False,
                                      "wo_residual_norm")

    taps_f = conv_ffn_w.shape[1]
    if is_sample:
        prev_f = _pad_history(group["state_f"][layer], V7X_SUBLANES)
        act, g_up = _ffn_up(h2, w_ffn_in, conv_ffn_w, layer, prev_f, seq, tm)
        new_f = jnp.concatenate([group["state_f"][layer], g_up.reshape(batch, seq, d_ff)],
                                axis=1)[:, -(taps_f - 1):]
    else:
        tm_up = _tile(seq, FFN_UP_ROWS)
        act, g_tail = _ffn_up(h2, w_ffn_in, conv_ffn_w, layer, None, seq, tm_up)
        blocks_per_seq = seq // tm_up
        g_tail = g_tail.reshape(batch, blocks_per_seq, V7X_SUBLANES, d_ff)
        new_f = g_tail[:, blocks_per_seq - 1, V7X_SUBLANES - (taps_f - 1):]
    x_new, h_next = _residual_matmul_norm(x_mid, act, w_ffn_down, layer, g_next, tm, final,
                                          "ffn_down_residual_norm")
    k_out = k_f.reshape(batch, seq, n_heads, HEAD_W)
    v_out = v_f.reshape(batch, seq, n_heads, HEAD_W)
    return x_new, h_next, k_out, v_out, new_a, new_c, new_f


def kernel(x_prompt, x_sample, cache_k, cache_v, state_conv_a, state_conv_c, state_conv_ffn,
           page_table, norm_mix, w_in, conv_a_w, conv_a_b, ln_a_g, ln_a_b, w_a_out, lam_q1,
           lam_k1, lam_q2, lam_k2, subln_g, w_b_out, conv_c_w, w_c_out, w_o, norm_ffn, w_ffn_in,
           conv_ffn_w, w_ffn_down, norm_final):
    bp, t_p, d = x_prompt.shape
    bs, t_s, _ = x_sample.shape
    depth = w_in.shape[0]
    past = page_table.shape[1] * PAGE_SIZE

    tm_p = _tile(t_p, 1024)
    cos_p, sin_p = _rope_tables(jnp.arange(t_p, dtype=jnp.int32))
    cos_s, sin_s = _rope_tables(past + jnp.arange(t_s, dtype=jnp.int32))
    groups = [
        dict(batch=bp, seq=t_p, tm=tm_p, tt=_tile(t_p, 256),
             rope=(cos_p, sin_p), cache=None),
        dict(batch=bs, seq=t_s, tm=bs * t_s, tt=t_s,
             rope=(jnp.tile(cos_s, (bs, 1)), jnp.tile(sin_s, (bs, 1))),
             cache=(cache_k, cache_v, page_table),
             state_a=state_conv_a, state_c=state_conv_c, state_f=state_conv_ffn),
    ]
    xs = [x_prompt.reshape(bp * t_p, d), x_sample.reshape(bs * t_s, d)]
    hs = [_rmsnorm_bf16(x, norm_mix[0]) for x in xs]
    collected = [[], []]
    for layer in range(depth):
        lam_init = 0.8 - 0.6 * math.exp(-0.3 * layer)
        final = layer == depth - 1
        g_next = norm_final if final else norm_mix[layer + 1]
        weights = (w_in, conv_a_w, conv_a_b, ln_a_g, ln_a_b, w_a_out, lam_q1, lam_k1, lam_q2,
                   lam_k2, subln_g, w_b_out, conv_c_w, w_c_out, w_o, norm_ffn, w_ffn_in,
                   conv_ffn_w, w_ffn_down, g_next)
        for gi, group in enumerate(groups):
            x_new, h_next, k_out, v_out, new_a, new_c, new_f = _layer(
                xs[gi], hs[gi], layer, lam_init, final, group, weights)
            xs[gi], hs[gi] = x_new, h_next
            collected[gi].append((k_out, v_out, new_a, new_c, new_f))
    y_prompt = xs[0].reshape(bp, t_p, d)
    y_sample = xs[1].reshape(bs, t_s, d)
    stack = lambda gi, idx: jnp.stack([c[idx] for c in collected[gi]])
    return (y_prompt, y_sample,
            stack(0, 0), stack(0, 1), stack(0, 2), stack(0, 3), stack(0, 4),
            stack(1, 0), stack(1, 1), stack(1, 2), stack(1, 3), stack(1, 4))
```

```python
import functools
import math

import jax
import jax.numpy as jnp
from jax import lax
from jax.experimental import pallas as pl
from jax.experimental.pallas import tpu as pltpu

BF16 = jnp.bfloat16
F32 = jnp.float32

HEAD_DIM = 64
PAGE_SIZE = 128
ROPE_THETA = 10000.0
RMS_EPS = 1e-6
LN_EPS = 1e-5
SUBLN_EPS = 1e-5
NEG_INF = -1e30

V7X_LANES = 128
V7X_SUBLANES = 8
V7X_VMEM_LIMIT_BYTES = 60 * 1024 * 1024

HEAD_W = 2 * HEAD_DIM


def _tile(dim, pref):
    if dim <= pref:
        return dim
    t = pref
    while t >= V7X_SUBLANES:
        if dim % t == 0 and t % V7X_SUBLANES == 0:
            return t
        t -= V7X_SUBLANES
    return dim


def _tile_count(total, pref):
    return max(c for c in range(1, min(total, pref) + 1) if total % c == 0)


SAMPLE_PAGES_PER_STEP = 8
SAMPLE_SEQS_PER_STEP = 2
PROMPT_ATTN_BLOCK = 512
PROMPT_ATTN_HEADS_PER_STEP = 2
FFN_UP_ROWS = 1024
LOG2_E = 1.4426950408889634


def _params(sem):
    return pltpu.CompilerParams(dimension_semantics=sem,
                                vmem_limit_bytes=V7X_VMEM_LIMIT_BYTES)


def _dot(a, b):
    return jnp.dot(a, b, preferred_element_type=F32)


def _dot_nt(a, b):
    return lax.dot_general(a, b, (((1,), (1,)), ((), ())), preferred_element_type=F32)


def _rms(x, g, eps):
    return x * lax.rsqrt(jnp.mean(x * x, axis=-1, keepdims=True) + eps) * g


def _rmsnorm_kernel(x_ref, g_ref, o_ref):
    o_ref[...] = _rms(x_ref[...], g_ref[...], RMS_EPS).astype(o_ref.dtype)


def _rmsnorm_bf16(x, g):
    m, d = x.shape
    tm = _tile(m, 512)
    return pl.pallas_call(
        _rmsnorm_kernel,
        grid=(m // tm,),
        in_specs=[pl.BlockSpec((tm, d), lambda i: (i, 0)),
                  pl.BlockSpec((1, d), lambda i: (0, 0))],
        out_specs=pl.BlockSpec((tm, d), lambda i: (i, 0)),
        out_shape=jax.ShapeDtypeStruct((m, d), BF16),
        compiler_params=_params(("parallel",)),
        name="rmsnorm_in",
    )(x, g.reshape(1, d))


def _w_spec(d, tn, layer, col_off):
    assert col_off % tn == 0
    base = col_off // tn
    return pl.BlockSpec((None, d, tn), lambda i, j: (layer, 0, base + j))


def _glu_kernel(h_ref, wv_ref, wg_ref, o_ref):
    h = h_ref[...]
    val = _dot(h, wv_ref[...].astype(BF16))
    gate = _dot(h, wg_ref[...].astype(BF16))
    o_ref[...] = val * jax.nn.sigmoid(gate)


def _proj_glu(h, w_in, layer, off_val, off_gate, width, tm):
    m, d = h.shape
    tn = _tile(width, 512)
    return pl.pallas_call(
        _glu_kernel,
        grid=(m // tm, width // tn),
        in_specs=[pl.BlockSpec((tm, d), lambda i, j: (i, 0)),
                  _w_spec(d, tn, layer, off_val),
                  _w_spec(d, tn, layer, off_gate)],
        out_specs=pl.BlockSpec((tm, tn), lambda i, j: (i, j)),
        out_shape=jax.ShapeDtypeStruct((m, width), F32),
        compiler_params=_params(("parallel", "arbitrary")),
        name="proj_glu",
    )(h, w_in, w_in)


def _rope_tile(x, cos, sin_signed):
    lane = lax.broadcasted_iota(jnp.int32, x.shape, 1)
    first_half = (lane % HEAD_DIM) < (HEAD_DIM // 2)
    partner = jnp.where(first_half,
                        pltpu.roll(x, V7X_LANES - HEAD_DIM // 2, 1),
                        pltpu.roll(x, HEAD_DIM // 2, 1))
    return x * cos + partner * sin_signed


def _heads_kernel(h_ref, w_ref, cos_ref, sin_ref, *o_refs, rope, scale, f32_layout, out_bf16,
                  transposed_tk):
    acc = _dot(h_ref[...], w_ref[...].astype(BF16))
    tm, width = acc.shape
    n_heads = width // HEAD_W
    for head in range(n_heads):
        sl = slice(head * HEAD_W, (head + 1) * HEAD_W)
        y = acc[:, sl]
        if rope:
            y = _rope_tile(y, cos_ref[...], sin_ref[...])
        if scale != 1.0:
            y = y * scale
        k = 0
        if f32_layout == "rows":
            o_refs[k][:, sl] = y
            k += 1
        elif f32_layout == "heads":
            o_refs[k][pl.ds(head, tm, stride=n_heads), :] = y
            k += 1
        if out_bf16:
            o_refs[k][:, sl] = y.astype(BF16)
            k += 1
        if transposed_tk:
            for c in range(tm // transposed_tk):
                rows = slice(c * transposed_tk, (c + 1) * transposed_tk)
                o_refs[k][head, c] = y[rows, :].T.astype(BF16)


def _proj_heads(h, w_in, layer, off, width, cos, sin, tm, *, rope, scale=1.0, f32_layout=None,
                out_bf16=False, transposed_tk=0, seq=None):
    m, d = h.shape
    n_heads = width // HEAD_W
    n_pos_blocks = cos.shape[0] // tm
    tab_spec = pl.BlockSpec((tm, V7X_LANES), lambda i: (i % n_pos_blocks, 0))
    out_specs, out_shape = [], []
    if f32_layout == "rows":
        out_specs.append(pl.BlockSpec((tm, width), lambda i: (i, 0)))
        out_shape.append(jax.ShapeDtypeStruct((m, width), F32))
    elif f32_layout == "heads":
        out_specs.append(pl.BlockSpec((tm * n_heads, HEAD_W), lambda i: (i, 0)))
        out_shape.append(jax.ShapeDtypeStruct((m * n_heads, HEAD_W), F32))
    if out_bf16:
        out_specs.append(pl.BlockSpec((tm, width), lambda i: (i, 0)))
        out_shape.append(jax.ShapeDtypeStruct((m, width), BF16))
    if transposed_tk:
        assert tm % transposed_tk == 0 and seq % tm == 0
        blocks_per_seq = seq // tm
        sub = tm // transposed_tk
        out_specs.append(pl.BlockSpec(
            (None, n_heads, sub, HEAD_W, transposed_tk),
            lambda i: (i // blocks_per_seq, 0, i % blocks_per_seq, 0, 0)))
        out_shape.append(jax.ShapeDtypeStruct(
            (m // seq, n_heads, seq // transposed_tk, HEAD_W, transposed_tk), BF16))
    assert off % width == 0
    return pl.pallas_call(
        functools.partial(_heads_kernel, rope=rope, scale=scale, f32_layout=f32_layout,
                          out_bf16=out_bf16, transposed_tk=transposed_tk),
        grid=(m // tm,),
        in_specs=[pl.BlockSpec((tm, d), lambda i: (i, 0)),
                  pl.BlockSpec((None, d, width), lambda i: (layer, 0, off // width)),
                  tab_spec, tab_spec],
        out_specs=out_specs,
        out_shape=out_shape,
        compiler_params=_params(("parallel",)),
        name="proj_heads",
    )(h, w_in, cos, sin)


def _cbranch_kernel(h_ref, wb_ref, wc_ref, wx_ref, cb_ref, cc_ref):
    h = h_ref[...]
    cb_ref[...] = _dot(h, wb_ref[...].astype(BF16)).astype(cb_ref.dtype)
    c_c = _dot(h, wc_ref[...].astype(BF16))
    c_x = _dot(h, wx_ref[...].astype(BF16))
    cc_ref[...] = c_c * c_x


def _proj_cbranch(h, w_in, layer, off_b, off_c, off_x, width, tm):
    m, d = h.shape
    tn = _tile(width, 512)
    return pl.pallas_call(
        _cbranch_kernel,
        grid=(m // tm, width // tn),
        in_specs=[pl.BlockSpec((tm, d), lambda i, j: (i, 0)),
                  _w_spec(d, tn, layer, off_b),
                  _w_spec(d, tn, layer, off_c),
                  _w_spec(d, tn, layer, off_x)],
        out_specs=[pl.BlockSpec((tm, tn), lambda i, j: (i, j)),
                   pl.BlockSpec((tm, tn), lambda i, j: (i, j))],
        out_shape=[jax.ShapeDtypeStruct((m, width), BF16),
                   jax.ShapeDtypeStruct((m, width), F32)],
        compiler_params=_params(("parallel", "arbitrary")),
        name="proj_cbranch",
    )(h, w_in, w_in, w_in)


def _dwconv_fill(u_ref, prev_ref, buf, hist_rows, tt):
    ti = pl.program_id(1)

    @pl.when(ti == 0)
    def _():
        buf[0:hist_rows, :] = prev_ref[...]

    @pl.when(ti > 0)
    def _():
        buf[0:hist_rows, :] = buf[tt:tt + hist_rows, :]

    buf[hist_rows:hist_rows + tt, :] = u_ref[...]


def _dwconv_rows(buf, w_ref, taps, hist_rows, r0, rows, c0, cols):
    base = hist_rows - (taps - 1)
    acc = None
    for k in range(taps):
        term = buf[r0 + base + k:r0 + base + k + rows, c0:c0 + cols] * w_ref[k:k + 1, c0:c0 + cols]
        acc = term if acc is None else acc + term
    return acc


def _conv_ln_silu_kernel(u_ref, prev_ref, w_ref, b_ref, g_ref, beta_ref, o_ref, buf, shifted,
                         ybuf, *, taps, hist_rows, tt, row_chunk, col_chunk):
    _dwconv_fill(u_ref, prev_ref, buf, hist_rows, tt)
    c = u_ref.shape[-1]
    base = hist_rows - (taps - 1)
    span = shifted.shape[1]
    for res in range(1, V7X_SUBLANES):
        shifted[res - 1] = buf[res:res + span, :]
    for r0 in range(0, tt, row_chunk):
        for c0 in range(0, c, col_chunk):
            acc = None
            for k in range(taps):
                res, tile_off = (base + k) % V7X_SUBLANES, (base + k) // V7X_SUBLANES * V7X_SUBLANES
                rows = slice(r0 + tile_off, r0 + tile_off + row_chunk)
                cols = slice(c0, c0 + col_chunk)
                window = buf[rows, cols] if res == 0 else shifted[res - 1, rows, cols]
                term = window * w_ref[k:k + 1, cols]
                acc = term if acc is None else acc + term
            ybuf[r0:r0 + row_chunk, c0:c0 + col_chunk] = acc
    y = ybuf[...] + b_ref[...]
    mu = jnp.mean(y, axis=-1, keepdims=True)
    yc = y - mu
    var = jnp.mean(yc * yc, axis=-1, keepdims=True)
    z = yc * lax.rsqrt(var + LN_EPS) * g_ref[...] + beta_ref[...]
    o_ref[...] = (z * jax.nn.sigmoid(z)).astype(o_ref.dtype)


def _conv_ln_silu(u, prev, w, bias, ln_g, ln_b, tt):
    b, t, c = u.shape
    taps = w.shape[0]
    hist_rows = prev.shape[1]
    row_chunk = min(tt, 32)
    col_chunk = min(c, 512)
    vec = lambda a: a.reshape(1, c)
    return pl.pallas_call(
        functools.partial(_conv_ln_silu_kernel, taps=taps, hist_rows=hist_rows, tt=tt,
                          row_chunk=row_chunk, col_chunk=col_chunk),
        grid=(b, t // tt),
        in_specs=[pl.BlockSpec((None, tt, c), lambda bi, ti: (bi, ti, 0)),
                  pl.BlockSpec((None, hist_rows, c), lambda bi, ti: (bi, 0, 0)),
                  pl.BlockSpec((taps, c), lambda bi, ti: (0, 0)),
                  pl.BlockSpec((1, c), lambda bi, ti: (0, 0)),
                  pl.BlockSpec((1, c), lambda bi, ti: (0, 0)),
                  pl.BlockSpec((1, c), lambda bi, ti: (0, 0))],
        out_specs=pl.BlockSpec((None, tt, c), lambda bi, ti: (bi, ti, 0)),
        out_shape=jax.ShapeDtypeStruct((b, t, c), BF16),
        scratch_shapes=[pltpu.VMEM((hist_rows + tt, c), F32),
                        pltpu.VMEM((V7X_SUBLANES - 1, tt + hist_rows - V7X_SUBLANES, c), F32),
                        pltpu.VMEM((tt, c), F32)],
        compiler_params=_params(("parallel", "arbitrary")),
        name="conv_a",
    )(u, prev, w, vec(bias), vec(ln_g), vec(ln_b))


def _conv_mul_kernel(u_ref, prev_ref, w_ref, cb_ref, o_ref, buf, *, taps, hist_rows, tt):
    _dwconv_fill(u_ref, prev_ref, buf, hist_rows, tt)
    c = u_ref.shape[-1]
    conv = _dwconv_rows(buf, w_ref, taps, hist_rows, 0, tt, 0, c)
    o_ref[...] = (cb_ref[...].astype(F32) * conv).astype(o_ref.dtype)


def _conv_mul(u, prev, w, cb, tt):
    b, t, c = u.shape
    taps = w.shape[0]
    hist_rows = prev.shape[1]
    return pl.pallas_call(
        functools.partial(_conv_mul_kernel, taps=taps, hist_rows=hist_rows, tt=tt),
        grid=(b, t // tt),
        in_specs=[pl.BlockSpec((None, tt, c), lambda bi, ti: (bi, ti, 0)),
                  pl.BlockSpec((None, hist_rows, c), lambda bi, ti: (bi, 0, 0)),
                  pl.BlockSpec((taps, c), lambda bi, ti: (0, 0)),
                  pl.BlockSpec((None, tt, c), lambda bi, ti: (bi, ti, 0))],
        out_specs=pl.BlockSpec((None, tt, c), lambda bi, ti: (bi, ti, 0)),
        out_shape=jax.ShapeDtypeStruct((b, t, c), BF16),
        scratch_shapes=[pltpu.VMEM((hist_rows + tt, c), F32)],
        compiler_params=_params(("parallel", "arbitrary")),
        name="conv_c",
    )(u, prev, w, cb)


def _lambda(lq1_ref, lk1_ref, lq2_ref, lk2_ref, lam_init):
    s1 = jnp.sum(lq1_ref[...] * lk1_ref[...], axis=-1, keepdims=True)
    s2 = jnp.sum(lq2_ref[...] * lk2_ref[...], axis=-1, keepdims=True)
    return jnp.exp(s1) - jnp.exp(s2) + lam_init


def _subln(o, g, lam_init):
    return _rms(o, g, SUBLN_EPS) * (1.0 - lam_init)


def _attn_prompt_kernel(q_ref, k_ref, vt_ref, lq1_ref, lk1_ref, lq2_ref, lk2_ref, gs_ref,
                        o_ref, s_ref, m_ref, l_ref, acc_ref, *, tb, lam_init, heads):
    qi = pl.program_id(2)
    lane = lax.broadcasted_iota(jnp.int32, (tb, HEAD_W), 1)
    q_maps = []
    for hh in range(heads):
        q = q_ref[:, hh * HEAD_W:(hh + 1) * HEAD_W]
        zero = jnp.zeros_like(q)
        q_maps.append((jnp.where(lane < HEAD_DIM, q, zero), jnp.where(lane >= HEAD_DIM, q, zero)))
    m_ref[...] = jnp.full(m_ref.shape, NEG_INF, F32)
    l_ref[...] = jnp.zeros(l_ref.shape, F32)
    acc_ref[...] = jnp.zeros(acc_ref.shape, F32)

    def scores(kj, slot):
        rows = pl.ds(pl.multiple_of(kj * tb, tb), tb)
        for hh in range(heads):
            k = k_ref[rows, hh * HEAD_W:(hh + 1) * HEAD_W]
            for mp in range(2):
                s_ref[slot, hh, mp] = _dot_nt(k, q_maps[hh][mp])

    def consume(kj, slot, masked):
        for hh in range(heads):
            vt = vt_ref[hh, kj]
            for mp in range(2):
                s = s_ref[slot, hh, mp]
                if masked:
                    key = lax.broadcasted_iota(jnp.int32, s.shape, 0)
                    qry = lax.broadcasted_iota(jnp.int32, s.shape, 1)
                    s = jnp.where(key <= qry, s, NEG_INF)
                m_prev = m_ref[hh, mp]
                m_new = jnp.maximum(m_prev, jnp.max(s, axis=0, keepdims=True))
                alpha = jnp.exp2(m_prev - m_new)
                p = jnp.exp2(s - m_new)
                l_ref[hh, mp] = alpha * l_ref[hh, mp] + jnp.sum(p, axis=0, keepdims=True)
                acc_ref[hh, mp] = alpha * acc_ref[hh, mp] + _dot(vt, p.astype(BF16))
                m_ref[hh, mp] = m_new

    scores(0, 0)

    def block_pair(t, carry):
        scores(2 * t + 1, 1)
        consume(2 * t, 0, False)
        scores(2 * t + 2, 0)
        consume(2 * t + 1, 1, False)
        return carry

    lax.fori_loop(0, qi // 2, block_pair, 0)

    @pl.when(qi % 2 == 0)
    def _():
        consume(qi, 0, True)

    @pl.when(qi % 2 == 1)
    def _():
        scores(qi, 1)
        consume(qi - 1, 0, False)
        consume(qi, 1, True)

    lam = _lambda(lq1_ref, lk1_ref, lq2_ref, lk2_ref, lam_init)
    for hh in range(heads):
        diff = acc_ref[hh, 0] / l_ref[hh, 0] - lam * (acc_ref[hh, 1] / l_ref[hh, 1])
        ms = jnp.mean(diff * diff, axis=0, keepdims=True)
        normed = diff * lax.rsqrt(ms + SUBLN_EPS) * gs_ref[...] * (1.0 - lam_init)
        o_ref[:, hh * HEAD_W:(hh + 1) * HEAD_W] = normed.T.astype(o_ref.dtype)


def _attn_prompt(qb, kb, vt, lam_params, g_sub, lam_init, batch, seq, tb):
    m, w = qb.shape
    n_heads = w // HEAD_W
    heads = _tile_count(n_heads, PROMPT_ATTN_HEADS_PER_STEP)
    hw = heads * HEAD_W
    nq = seq // tb
    small = lambda n: pl.BlockSpec((1, n), lambda b, h, qi: (0, 0))
    return pl.pallas_call(
        functools.partial(_attn_prompt_kernel, tb=tb, lam_init=lam_init, heads=heads),
        grid=(batch, n_heads // heads, nq),
        in_specs=[pl.BlockSpec((tb, hw), lambda b, h, qi: (b * nq + qi, h)),
                  pl.BlockSpec((seq, hw), lambda b, h, qi: (b, h)),
                  pl.BlockSpec((None, heads, nq, HEAD_W, tb), lambda b, h, qi: (b, h, 0, 0, 0)),
                  small(HEAD_DIM), small(HEAD_DIM), small(HEAD_DIM), small(HEAD_DIM),
                  pl.BlockSpec((HEAD_W, 1), lambda b, h, qi: (0, 0))],
        out_specs=pl.BlockSpec((tb, hw), lambda b, h, qi: (b * nq + qi, h)),
        out_shape=jax.ShapeDtypeStruct((m, w), BF16),
        scratch_shapes=[pltpu.VMEM((2, heads, 2, tb, tb), F32),
                        pltpu.VMEM((heads, 2, 1, tb), F32),
                        pltpu.VMEM((heads, 2, 1, tb), F32),
                        pltpu.VMEM((heads, 2, HEAD_W, tb), F32)],
        compiler_params=_params(("parallel", "parallel", "arbitrary")),
        name="attn_prompt",
    )(qb, kb, vt, *lam_params, g_sub.reshape(HEAD_W, 1))


def _attn_sample_kernel(pt_ref, q_ref, kn_ref, vn_ref, *refs, t_new, n_heads, lam_init, n_pages,
                        seqs):
    page_refs = refs[:2 * seqs * n_pages]
    (lq1_ref, lk1_ref, lq2_ref, lk2_ref, gs_ref, o_ref,
     qbd_ref, bias_ref, m_ref, l_ref, acc_ref) = refs[2 * seqs * n_pages:]
    del pt_ref
    p = pl.program_id(1)
    rows_per_head = 2 * t_new

    @pl.when(p == 0)
    def _():
        lane = lax.broadcasted_iota(jnp.int32, (t_new, HEAD_W), 1)
        for r in range(seqs):
            q = q_ref[r]
            for h in range(n_heads):
                qh = q[:, h * HEAD_W:(h + 1) * HEAD_W]
                r0 = h * rows_per_head
                qbd_ref[r, r0:r0 + t_new, :] = jnp.where(lane < HEAD_DIM, qh, 0.0)
                qbd_ref[r, r0 + t_new:r0 + rows_per_head, :] = jnp.where(lane >= HEAD_DIM, qh, 0.0)
        m_ref[...] = jnp.full(m_ref.shape, NEG_INF, F32)
        l_ref[...] = jnp.zeros(l_ref.shape, F32)
        acc_ref[...] = jnp.zeros(acc_ref.shape, F32)

        row = lax.broadcasted_iota(jnp.int32, bias_ref.shape, 0)
        col = lax.broadcasted_iota(jnp.int32, bias_ref.shape, 1)
        bias_ref[...] = jnp.where(col % n_heads == row // rows_per_head, 0.0, NEG_INF)

    def update(r, scores, values):
        m_prev = m_ref[r]
        m_new = m_prev
        for s in scores:
            m_new = jnp.maximum(m_new, jnp.max(s, axis=-1, keepdims=True))
        alpha = jnp.exp2(m_prev - m_new)
        l_new = alpha * l_ref[r]
        acc = alpha * acc_ref[r]
        for s, v in zip(scores, values):
            pr = jnp.exp2(s - m_new)
            l_new = l_new + jnp.sum(pr, axis=-1, keepdims=True)
            acc = acc + _dot(pr.astype(BF16), v.astype(BF16))
        l_ref[r] = l_new
        acc_ref[r] = acc
        m_ref[r] = m_new

    page_rows = PAGE_SIZE * n_heads
    bias = bias_ref[...]
    for r in range(seqs):
        qbd = qbd_ref[r].astype(BF16)
        k_refs = page_refs[r * n_pages:(r + 1) * n_pages]
        v_refs = page_refs[(seqs + r) * n_pages:(seqs + r + 1) * n_pages]
        scores = [_dot_nt(qbd, k_ref[...].reshape(page_rows, HEAD_W).astype(BF16)) + bias
                  for k_ref in k_refs]
        values = [v_ref[...].reshape(page_rows, HEAD_W) for v_ref in v_refs]
        update(r, scores, values)

    @pl.when(p == pl.num_programs(1) - 1)
    def _():
        lam = _lambda(lq1_ref, lk1_ref, lq2_ref, lk2_ref, lam_init)
        for r in range(seqs):
            s = _dot_nt(qbd_ref[r].astype(BF16), kn_ref[r].astype(BF16))
            row = lax.broadcasted_iota(jnp.int32, s.shape, 0)
            col = lax.broadcasted_iota(jnp.int32, s.shape, 1)
            valid = jnp.logical_and(col % n_heads == row // rows_per_head,
                                    col // n_heads <= row % t_new)
            update(r, [jnp.where(valid, s, NEG_INF)], [vn_ref[r]])
            o = acc_ref[r] / l_ref[r]
            for h in range(n_heads):
                r0 = h * rows_per_head
                diff = o[r0:r0 + t_new, :] - lam * o[r0 + t_new:r0 + rows_per_head, :]
                o_ref[r, :, h * HEAD_W:(h + 1) * HEAD_W] = _subln(
                    diff, gs_ref[...], lam_init).astype(o_ref.dtype)


def _attn_sample(q, k_new, v_new, cache_k, cache_v, page_table, layer, lam_params, g_sub,
                 lam_init):
    b, t_new, w = q.shape
    n_heads = w // HEAD_W
    total_pages = page_table.shape[1]
    n_pages = _tile_count(total_pages, SAMPLE_PAGES_PER_STEP)
    seqs = _tile_count(b, SAMPLE_SEQS_PER_STEP)
    assert t_new % V7X_SUBLANES == 0
    rows = 2 * n_heads * t_new

    q_spec = pl.BlockSpec((seqs, t_new, w), lambda bi, p, pt: (bi, 0, 0))
    new_spec = pl.BlockSpec((seqs, t_new * n_heads, HEAD_W), lambda bi, p, pt: (bi, 0, 0))
    page_shape = (None, None, PAGE_SIZE, n_heads, HEAD_W)
    page_specs = [
        pl.BlockSpec(page_shape, lambda bi, p, pt, r=r, c=c: (
            layer, pt[bi * seqs + r, n_pages * p + c], 0, 0, 0))
        for r in range(seqs) for c in range(n_pages)]
    small = lambda n: pl.BlockSpec((1, n), lambda bi, p, pt: (0, 0))
    grid_spec = pltpu.PrefetchScalarGridSpec(
        num_scalar_prefetch=1,
        grid=(b // seqs, total_pages // n_pages),
        in_specs=[q_spec, new_spec, new_spec] + page_specs + page_specs
                 + [small(HEAD_DIM), small(HEAD_DIM), small(HEAD_DIM), small(HEAD_DIM),
                    small(HEAD_W)],
        out_specs=pl.BlockSpec((seqs, t_new, w), lambda bi, p, pt: (bi, 0, 0)),
        scratch_shapes=[pltpu.VMEM((seqs, rows, HEAD_W), F32),
                        pltpu.VMEM((rows, PAGE_SIZE * n_heads), F32),
                        pltpu.VMEM((seqs, rows, 1), F32),
                        pltpu.VMEM((seqs, rows, 1), F32),
                        pltpu.VMEM((seqs, rows, HEAD_W), F32)],
    )
    n_page_ops = seqs * n_pages
    return pl.pallas_call(
        functools.partial(_attn_sample_kernel, t_new=t_new, n_heads=n_heads, lam_init=lam_init,
                          n_pages=n_pages, seqs=seqs),
        grid_spec=grid_spec,
        out_shape=jax.ShapeDtypeStruct((b, t_new, w), BF16),
        compiler_params=_params(("parallel", "arbitrary")),
        name="attn_sample",
    )(page_table, q, k_new, v_new, *([cache_k] * n_page_ops), *([cache_v] * n_page_ops),
      *lam_params, g_sub.reshape(1, HEAD_W))


def _merge_kernel(ta_ref, tb_ref, tc_ref, h_ref, wa_ref, wb_ref, wc_ref,
                  wga_ref, wgb_ref, wgc_ref, o_ref):
    h = h_ref[...]

    def branch(t_ref, w_ref, wg_ref):
        y = _dot(t_ref[...], w_ref[...].astype(BF16))
        g = _dot(h, wg_ref[...].astype(BF16))
        return jax.nn.sigmoid(g) * y

    merged = (branch(ta_ref, wa_ref, wga_ref) + branch(tb_ref, wb_ref, wgb_ref)
              + branch(tc_ref, wc_ref, wgc_ref))
    o_ref[...] = merged.astype(o_ref.dtype)


def _merge(ta, tb, tc, h, w_a_out, w_b_out, w_c_out, w_in, layer, gate_offs, tm):
    m, d = h.shape
    tn = _tile(d, 256)
    act = lambda a: pl.BlockSpec((tm, a.shape[1]), lambda i, j: (i, 0))
    wout = lambda a: pl.BlockSpec((None, a.shape[1], tn), lambda i, j: (layer, 0, j))
    return pl.pallas_call(
        _merge_kernel,
        grid=(m // tm, d // tn),
        in_specs=[act(ta), act(tb), act(tc), act(h),
                  wout(w_a_out), wout(w_b_out), wout(w_c_out),
                  _w_spec(d, tn, layer, gate_offs[0]),
                  _w_spec(d, tn, layer, gate_offs[1]),
                  _w_spec(d, tn, layer, gate_offs[2])],
        out_specs=pl.BlockSpec((tm, tn), lambda i, j: (i, j)),
        out_shape=jax.ShapeDtypeStruct((m, d), BF16),
        compiler_params=_params(("parallel", "arbitrary")),
        name="merge",
    )(ta, tb, tc, h, w_a_out, w_b_out, w_c_out, w_in, w_in, w_in)


def _residual_kernel(x_ref, lhs_ref, w_ref, g_ref, xo_ref, *h_refs, col_chunk, final):
    k = pl.program_id(1)

    @pl.when(k == 0)
    def _():
        xo_ref[...] = x_ref[...]

    lhs = lhs_ref[...]
    for c0 in range(0, xo_ref.shape[1], col_chunk):
        xo_ref[:, c0:c0 + col_chunk] += _dot(lhs, w_ref[:, c0:c0 + col_chunk].astype(BF16))

    @pl.when(k == pl.num_programs(1) - 1)
    def _():
        y = _rms(xo_ref[...], g_ref[...], RMS_EPS)
        if final:
            xo_ref[...] = y
        else:
            h_refs[0][...] = y.astype(h_refs[0].dtype)


def _residual_matmul_norm(x, lhs, w, layer, g_next, tm, final, name):
    m, d = x.shape
    kdim = lhs.shape[1]
    tk = _tile(kdim, 512)
    row_spec = pl.BlockSpec((tm, d), lambda i, k: (i, 0))
    out_specs = [row_spec] + ([] if final else [row_spec])
    out_shape = ([jax.ShapeDtypeStruct((m, d), F32)]
                 + ([] if final else [jax.ShapeDtypeStruct((m, d), BF16)]))
    outs = pl.pallas_call(
        functools.partial(_residual_kernel, col_chunk=_tile(d, 512), final=final),
        grid=(m // tm, kdim // tk),
        in_specs=[row_spec,
                  pl.BlockSpec((tm, tk), lambda i, k: (i, k)),
                  pl.BlockSpec((None, tk, d), lambda i, k: (layer, k, 0)),
                  pl.BlockSpec((1, d), lambda i, k: (0, 0))],
        out_specs=out_specs,
        out_shape=out_shape,
        compiler_params=_params(("parallel", "arbitrary")),
        name=name,
    )(x, lhs, w, g_next.reshape(1, d))
    return (outs[0], None) if final else tuple(outs)


def _ffn_up_kernel(h_ref, wg_ref, wv_ref, wc_ref, prev_ref, act_ref, gt_ref, buf_ref, hist_ref,
                   *, stream, seq_len, blocks_per_seq, sub):
    i = pl.program_id(0)
    j = pl.program_id(1)
    h = h_ref[...]
    tm = h.shape[0]
    fc = act_ref.shape[1]
    hist = V7X_SUBLANES
    nseq = tm // seq_len

    if stream:
        @pl.when(i % blocks_per_seq == 0)
        def _():
            buf_ref[:, 0:hist, :] = jnp.zeros((1, hist, fc), F32)

        @pl.when(i % blocks_per_seq != 0)
        def _():
            buf_ref[:, 0:hist, :] = hist_ref[j]
    else:
        buf_ref[:, 0:hist, :] = prev_ref[...]

    for c0 in range(0, fc, sub):
        cols = slice(c0, c0 + sub)
        gate = _dot(h, wg_ref[:, cols].astype(BF16))
        val = _dot(h, wv_ref[:, cols].astype(BF16))
        w = wc_ref[:, cols]
        if stream:
            hist_ref[j, :, :, cols] = gate[tm - hist:tm, :].reshape(1, hist, sub)
            gt_ref[:, cols] = gate[tm - hist:tm, :]
            prev_rows = buf_ref[0, 0:hist, cols]
            row = lax.broadcasted_iota(jnp.int32, (hist, sub), 0)
            shifted = []
            for shift in (2, 1):
                rolled = pltpu.roll(gate, shift, 0)
                top = jnp.where(row < shift, pltpu.roll(prev_rows, shift, 0), rolled[0:hist, :])
                shifted.append((top, rolled))

            def gated(rows):
                g2 = shifted[0][0] if rows is None else shifted[0][1][rows, :]
                g1 = shifted[1][0] if rows is None else shifted[1][1][rows, :]
                sl = slice(0, hist) if rows is None else rows
                conv = g2 * w[0:1, :] + g1 * w[1:2, :] + gate[sl, :] * w[2:3, :]
                return (conv * jax.nn.sigmoid(conv) * val[sl, :]).astype(act_ref.dtype)

            act_ref[0:hist, cols] = gated(None)
            act_ref[hist:tm, cols] = gated(slice(hist, tm))
        else:
            gt_ref[:, cols] = gate
            buf_ref[:, hist:hist + seq_len, cols] = gate.reshape(nseq, seq_len, sub)
            conv = (buf_ref[:, hist - 2:hist - 2 + seq_len, cols] * w[0:1, :].reshape(1, 1, sub)
                    + buf_ref[:, hist - 1:hist - 1 + seq_len, cols] * w[1:2, :].reshape(1, 1, sub)
                    + buf_ref[:, hist:hist + seq_len, cols] * w[2:3, :].reshape(1, 1, sub))
            conv = conv.reshape(tm, sub)
            act_ref[:, cols] = (conv * jax.nn.sigmoid(conv) * val).astype(act_ref.dtype)


def _ffn_up(h2, w_ffn_in, conv_w, layer, prev, seq_len, tm):
    m, d = h2.shape
    dff = conv_w.shape[-1]
    fc = _tile(dff, 512)
    nj = dff // fc
    stream = prev is None
    if stream:
        assert seq_len % tm == 0
        blocks_per_seq = seq_len // tm
        rows_in_block = tm
        prev = jnp.zeros((1, V7X_SUBLANES, dff), F32)
        prev_spec = pl.BlockSpec((1, V7X_SUBLANES, fc), lambda i, j: (0, 0, j))
        gt_shape = jax.ShapeDtypeStruct((m // tm, V7X_SUBLANES, dff), F32)
        gt_spec = pl.BlockSpec((None, V7X_SUBLANES, fc), lambda i, j: (i, 0, j))
        nseq = 1
    else:
        assert tm % seq_len == 0 and m == tm
        blocks_per_seq = 1
        rows_in_block = seq_len
        nseq = tm // seq_len
        prev_spec = pl.BlockSpec((nseq, V7X_SUBLANES, fc), lambda i, j: (0, 0, j))
        gt_shape = jax.ShapeDtypeStruct((m, dff), F32)
        gt_spec = pl.BlockSpec((tm, fc), lambda i, j: (i, j))
    return pl.pallas_call(
        functools.partial(_ffn_up_kernel, stream=stream, seq_len=rows_in_block,
                          blocks_per_seq=blocks_per_seq,
                          sub=fc // 2 if fc % (4 * V7X_LANES) == 0 else fc),
        grid=(m // tm, nj),
        in_specs=[pl.BlockSpec((tm, d), lambda i, j: (i, 0)),
                  pl.BlockSpec((None, d, fc), lambda i, j: (layer, 0, j)),
                  pl.BlockSpec((None, d, fc), lambda i, j: (layer, 0, nj + j)),
                  pl.BlockSpec((None, conv_w.shape[1], fc), lambda i, j: (layer, 0, j)),
                  prev_spec],
        out_specs=[pl.BlockSpec((tm, fc), lambda i, j: (i, j)), gt_spec],
        out_shape=[jax.ShapeDtypeStruct((m, dff), BF16), gt_shape],
        scratch_shapes=[pltpu.VMEM((nseq, V7X_SUBLANES + rows_in_block, fc), F32),
                        pltpu.VMEM((nj, 1, V7X_SUBLANES, fc), F32)],
        compiler_params=_params(("arbitrary", "arbitrary")),
        name="ffn_up",
    )(h2, w_ffn_in, w_ffn_in, conv_w, prev)


def _rope_tables(pos):
    half = HEAD_DIM // 2
    inv_freq = 1.0 / (ROPE_THETA ** (jnp.arange(half, dtype=F32) * 2.0 / HEAD_DIM))
    ang = pos.astype(F32)[:, None] * inv_freq[None, :]
    cos, sin = jnp.cos(ang), jnp.sin(ang)
    reps = V7X_LANES // HEAD_DIM
    cos_t = jnp.concatenate([cos, cos] * reps, axis=1)
    sin_t = jnp.concatenate([-sin, sin] * reps, axis=1)
    return cos_t, sin_t


def _pad_history(state, rows):
    b, k1, c = state.shape
    return jnp.concatenate([jnp.zeros((b, rows - k1, c), state.dtype), state], axis=1)


def _layer(x, h, layer, lam_init, final, group, weights):
    (w_in, conv_a_w, conv_a_b, ln_a_g, ln_a_b, w_a_out, lam_q1, lam_k1, lam_q2, lam_k2,
     subln_g, w_b_out, conv_c_w, w_c_out, w_o, norm_ffn, w_ffn_in, conv_ffn_w, w_ffn_down,
     g_next) = weights
    batch, seq, tm = group["batch"], group["seq"], group["tm"]
    m, d = x.shape
    d_a = conv_a_w.shape[-1]
    d_c = conv_c_w.shape[-1]
    d_ff = conv_ffn_w.shape[-1]
    qk_w = w_b_out.shape[1]
    taps_a, taps_c = conv_a_w.shape[1], conv_c_w.shape[1]
    offs = {}
    off = 0
    for name, width in (("a_val", d_a), ("a_gate", d_a), ("q", qk_w), ("k", qk_w), ("v", qk_w),
                        ("c_b", d_c), ("c_c", d_c), ("c_x", d_c),
                        ("g_a", d), ("g_b", d), ("g_c", d)):
        offs[name] = off
        off += width
    assert off == w_in.shape[-1]
    is_sample = group["cache"] is not None
    scale = HEAD_DIM ** -0.5 * LOG2_E
    lam_params = tuple(a[layer].reshape(1, HEAD_DIM) for a in (lam_q1, lam_k1, lam_q2, lam_k2))

    u_a = _proj_glu(h, w_in, layer, offs["a_val"], offs["a_gate"], d_a, tm)
    u_a3 = u_a.reshape(batch, seq, d_a)
    hist_a = -(-(taps_a - 1) // V7X_SUBLANES) * V7X_SUBLANES
    if is_sample:
        prev_a = _pad_history(group["state_a"][layer], hist_a)
        new_a = jnp.concatenate([group["state_a"][layer], u_a3], axis=1)[:, -(taps_a - 1):]
    else:
        prev_a = jnp.zeros((batch, hist_a, d_a), F32)
        new_a = u_a3[:, seq - (taps_a - 1):]
    t_a = _conv_ln_silu(u_a3, prev_a, conv_a_w[layer], conv_a_b[layer], ln_a_g[layer],
                        ln_a_b[layer], group["tt"]).reshape(m, d_a)

    cos, sin = group["rope"]
    n_heads = qk_w // HEAD_W
    proj = functools.partial(_proj_heads, h, w_in, layer, width=qk_w, cos=cos, sin=sin, tm=tm)
    if is_sample:
        (q_f,) = proj(off=offs["q"], rope=True, scale=scale, f32_layout="rows")
        (k_f,) = proj(off=offs["k"], rope=True, f32_layout="heads")
        (v_f,) = proj(off=offs["v"], rope=False, f32_layout="heads")
        cache_k, cache_v, page_table = group["cache"]
        t_b = _attn_sample(q_f.reshape(batch, seq, qk_w),
                           k_f.reshape(batch, seq * n_heads, HEAD_W),
                           v_f.reshape(batch, seq * n_heads, HEAD_W),
                           cache_k, cache_v, page_table, layer,
                           lam_params, subln_g[layer], lam_init).reshape(m, qk_w)
    else:
        tb = _tile(seq, PROMPT_ATTN_BLOCK)
        (q_b,) = proj(off=offs["q"], rope=True, scale=scale, out_bf16=True)
        k_f, k_b = proj(off=offs["k"], rope=True, f32_layout="heads", out_bf16=True)
        v_f, v_t = proj(off=offs["v"], rope=False, f32_layout="heads", transposed_tk=tb, seq=seq)
        t_b = _attn_prompt(q_b, k_b, v_t, lam_params, subln_g[layer], lam_init, batch, seq, tb)

    c_b, cc = _proj_cbranch(h, w_in, layer, offs["c_b"], offs["c_c"], offs["c_x"], d_c, tm)
    cc3 = cc.reshape(batch, seq, d_c)
    if is_sample:
        prev_c = _pad_history(group["state_c"][layer], V7X_SUBLANES)
        new_c = jnp.concatenate([group["state_c"][layer], cc3], axis=1)[:, -(taps_c - 1):]
    else:
        prev_c = jnp.zeros((batch, V7X_SUBLANES, d_c), F32)
        new_c = cc3[:, seq - (taps_c - 1):]
    t_c = _conv_mul(cc3, prev_c, conv_c_w[layer], c_b.reshape(batch, seq, d_c),
                    group["tt"]).reshape(m, d_c)

    merged = _merge(t_a, t_b, t_c, h, w_a_out, w_b_out, w_c_out, w_in, layer,
                    (offs["g_a"], offs["g_b"], offs["g_c"]), tm)
    x_mid, h2 = _residual_matmul_norm(x, merged, w_o, layer, norm_ffn[layer], tm, False,
                                      "wo_residual_norm")

    taps_f = conv_ffn_w.shape[1]
    if is_sample:
        prev_f = _pad_history(group["state_f"][layer], V7X_SUBLANES)
        act, g_up = _ffn_up(h2, w_ffn_in, conv_ffn_w, layer, prev_f, seq, tm)
        new_f = jnp.concatenate([group["state_f"][layer], g_up.reshape(batch, seq, d_ff)],
                                axis=1)[:, -(taps_f - 1):]
    else:
        tm_up = _tile(seq, FFN_UP_ROWS)
        act, g_tail = _ffn_up(h2, w_ffn_in, conv_ffn_w, layer, None, seq, tm_up)
        blocks_per_seq = seq // tm_up
        g_tail = g_tail.reshape(batch, blocks_per_seq, V7X_SUBLANES, d_ff)
        new_f = g_tail[:, blocks_per_seq - 1, V7X_SUBLANES - (taps_f - 1):]
    x_new, h_next = _residual_matmul_norm(x_mid, act, w_ffn_down, layer, g_next, tm, final,
                                          "ffn_down_residual_norm")
    k_out = k_f.reshape(batch, seq, n_heads, HEAD_W)
    v_out = v_f.reshape(batch, seq, n_heads, HEAD_W)
    return x_new, h_next, k_out, v_out, new_a, new_c, new_f


def kernel(x_prompt, x_sample, cache_k, cache_v, state_conv_a, state_conv_c, state_conv_ffn,
           page_table, norm_mix, w_in, conv_a_w, conv_a_b, ln_a_g, ln_a_b, w_a_out, lam_q1,
           lam_k1, lam_q2, lam_k2, subln_g, w_b_out, conv_c_w, w_c_out, w_o, norm_ffn, w_ffn_in,
           conv_ffn_w, w_ffn_down, norm_final):
    bp, t_p, d = x_prompt.shape
    bs, t_s, _ = x_sample.shape
    depth = w_in.shape[0]
    past = page_table.shape[1] * PAGE_SIZE

    tm_p = _tile(t_p, 1024)
    cos_p, sin_p = _rope_tables(jnp.arange(t_p, dtype=jnp.int32))
    cos_s, sin_s = _rope_tables(past + jnp.arange(t_s, dtype=jnp.int32))
    groups = [
        dict(batch=bp, seq=t_p, tm=tm_p, tt=_tile(t_p, 256),
             rope=(cos_p, sin_p), cache=None),
        dict(batch=bs, seq=t_s, tm=bs * t_s, tt=t_s,
             rope=(jnp.tile(cos_s, (bs, 1)), jnp.tile(sin_s, (bs, 1))),
             cache=(cache_k, cache_v, page_table),
             state_a=state_conv_a, state_c=state_conv_c, state_f=state_conv_ffn),
    ]
    xs = [x_prompt.reshape(bp * t_p, d), x_sample.reshape(bs * t_s, d)]
    hs = [_rmsnorm_bf16(x, norm_mix[0]) for x in xs]
    collected = [[], []]
    for layer in range(depth):
        lam_init = 0.8 - 0.6 * math.exp(-0.3 * layer)
        final = layer == depth - 1
        g_next = norm_final if final else norm_mix[layer + 1]
        weights = (w_in, conv_a_w, conv_a_b, ln_a_g, ln_a_b, w_a_out, lam_q1, lam_k1, lam_q2,
                   lam_k2, subln_g, w_b_out, conv_c_w, w_c_out, w_o, norm_ffn, w_ffn_in,
                   conv_ffn_w, w_ffn_down, g_next)
        for gi, group in enumerate(groups):
            x_new, h_next, k_out, v_out, new_a, new_c, new_f = _layer(
                xs[gi], hs[gi], layer, lam_init, final, group, weights)
            xs[gi], hs[gi] = x_new, h_next
            collected[gi].append((k_out, v_out, new_a, new_c, new_f))
    y_prompt = xs[0].reshape(bp, t_p, d)
    y_sample = xs[1].reshape(bs, t_s, d)
    stack = lambda gi, idx: jnp.stack([c[idx] for c in collected[gi]])
    return (y_prompt, y_sample,
            stack(0, 0), stack(0, 1), stack(0, 2), stack(0, 3), stack(0, 4),
            stack(1, 0), stack(1, 1), stack(1, 2), stack(1, 3), stack(1, 4))
```

```python
import functools
import math

import jax
import jax.numpy as jnp
from jax import lax
from jax.experimental import pallas as pl
from jax.experimental.pallas import tpu as pltpu

BF16 = jnp.bfloat16
F32 = jnp.float32

HEAD_DIM = 64
PAGE_SIZE = 128
ROPE_THETA = 10000.0
RMS_EPS = 1e-6
LN_EPS = 1e-5
SUBLN_EPS = 1e-5
NEG_INF = -1e30

V7X_LANES = 128
V7X_SUBLANES = 8
V7X_VMEM_LIMIT_BYTES = 60 * 1024 * 1024

HEAD_W = 2 * HEAD_DIM


def _tile(dim, pref):
    if dim <= pref:
        return dim
    t = pref
    while t >= V7X_SUBLANES:
        if dim % t == 0 and t % V7X_SUBLANES == 0:
            return t
        t -= V7X_SUBLANES
    return dim


def _tile_count(total, pref):
    return max(c for c in range(1, min(total, pref) + 1) if total % c == 0)


SAMPLE_PAGES_PER_STEP = 8
SAMPLE_SEQS_PER_STEP = 2
PROMPT_ATTN_BLOCK = 512
PROMPT_ATTN_HEADS_PER_STEP = 2
FFN_UP_ROWS = 1024
LOG2_E = 1.4426950408889634


def _params(sem):
    return pltpu.CompilerParams(dimension_semantics=sem,
                                vmem_limit_bytes=V7X_VMEM_LIMIT_BYTES)


def _dot(a, b):
    return jnp.dot(a, b, preferred_element_type=F32)


def _dot_nt(a, b):
    return lax.dot_general(a, b, (((1,), (1,)), ((), ())), preferred_element_type=F32)


def _rms(x, g, eps):
    return x * lax.rsqrt(jnp.mean(x * x, axis=-1, keepdims=True) + eps) * g


def _rmsnorm_kernel(x_ref, g_ref, o_ref):
    o_ref[...] = _rms(x_ref[...], g_ref[...], RMS_EPS).astype(o_ref.dtype)


def _rmsnorm_bf16(x, g):
    m, d = x.shape
    tm = _tile(m, 512)
    return pl.pallas_call(
        _rmsnorm_kernel,
        grid=(m // tm,),
        in_specs=[pl.BlockSpec((tm, d), lambda i: (i, 0)),
                  pl.BlockSpec((1, d), lambda i: (0, 0))],
        out_specs=pl.BlockSpec((tm, d), lambda i: (i, 0)),
        out_shape=jax.ShapeDtypeStruct((m, d), BF16),
        compiler_params=_params(("parallel",)),
        name="rmsnorm_in",
    )(x, g.reshape(1, d))


def _w_spec(d, tn, layer, col_off):
    assert col_off % tn == 0
    base = col_off // tn
    return pl.BlockSpec((None, d, tn), lambda i, j: (layer, 0, base + j))


def _glu_kernel(h_ref, wv_ref, wg_ref, o_ref):
    h = h_ref[...]
    val = _dot(h, wv_ref[...].astype(BF16))
    gate = _dot(h, wg_ref[...].astype(BF16))
    o_ref[...] = val * jax.nn.sigmoid(gate)


def _proj_glu(h, w_in, layer, off_val, off_gate, width, tm):
    m, d = h.shape
    tn = _tile(width, 512)
    return pl.pallas_call(
        _glu_kernel,
        grid=(m // tm, width // tn),
        in_specs=[pl.BlockSpec((tm, d), lambda i, j: (i, 0)),
                  _w_spec(d, tn, layer, off_val),
                  _w_spec(d, tn, layer, off_gate)],
        out_specs=pl.BlockSpec((tm, tn), lambda i, j: (i, j)),
        out_shape=jax.ShapeDtypeStruct((m, width), F32),
        compiler_params=_params(("parallel", "arbitrary")),
        name="proj_glu",
    )(h, w_in, w_in)


def _rope_tile(x, cos, sin_signed):
    lane = lax.broadcasted_iota(jnp.int32, x.shape, 1)
    first_half = (lane % HEAD_DIM) < (HEAD_DIM // 2)
    partner = jnp.where(first_half,
                        pltpu.roll(x, V7X_LANES - HEAD_DIM // 2, 1),
                        pltpu.roll(x, HEAD_DIM // 2, 1))
    return x * cos + partner * sin_signed


def _heads_kernel(h_ref, w_ref, cos_ref, sin_ref, *o_refs, rope, scale, f32_layout, out_bf16,
                  transposed_tk, aliased, zero_fill):
    if aliased:
        o_refs = o_refs[1:]
    if zero_fill is not None:
        for slot in range(o_refs[0].shape[0]):
            if slot != zero_fill:
                o_refs[0][slot] = jnp.zeros(o_refs[0].shape[1:], F32)
    acc = _dot(h_ref[...], w_ref[...].astype(BF16))
    tm, width = acc.shape
    n_heads = width // HEAD_W
    for head in range(n_heads):
        sl = slice(head * HEAD_W, (head + 1) * HEAD_W)
        y = acc[:, sl]
        if rope:
            y = _rope_tile(y, cos_ref[...], sin_ref[...])
        if scale != 1.0:
            y = y * scale
        k = 0
        if f32_layout == "rows":
            o_refs[k][:, sl] = y
            k += 1
        elif f32_layout == "heads" and zero_fill is not None:
            o_refs[k][zero_fill, pl.ds(head, tm, stride=n_heads), :] = y
            k += 1
        elif f32_layout == "heads":
            o_refs[k][pl.ds(head, tm, stride=n_heads), :] = y
            k += 1
        if out_bf16:
            o_refs[k][:, sl] = y.astype(BF16)
            k += 1
        if transposed_tk:
            for c in range(tm // transposed_tk):
                rows = slice(c * transposed_tk, (c + 1) * transposed_tk)
                o_refs[k][head, c] = y[rows, :].T.astype(BF16)


def _proj_heads(h, w_in, layer, off, width, cos, sin, tm, *, rope, scale=1.0, f32_layout=None,
                out_bf16=False, transposed_tk=0, seq=None, stack=None):
    m, d = h.shape
    n_heads = width // HEAD_W
    n_pos_blocks = cos.shape[0] // tm
    tab_spec = pl.BlockSpec((tm, V7X_LANES), lambda i: (i % n_pos_blocks, 0))
    out_specs, out_shape = [], []
    if f32_layout == "rows":
        out_specs.append(pl.BlockSpec((tm, width), lambda i: (i, 0)))
        out_shape.append(jax.ShapeDtypeStruct((m, width), F32))
    elif f32_layout == "heads" and stack is not None and stack[1] is None:
        out_specs.append(pl.BlockSpec((stack[0], tm * n_heads, HEAD_W), lambda i: (0, i, 0)))
        out_shape.append(jax.ShapeDtypeStruct((stack[0], m * n_heads, HEAD_W), F32))
    elif f32_layout == "heads" and stack is not None:
        out_specs.append(pl.BlockSpec((None, tm * n_heads, HEAD_W), lambda i: (layer, i, 0)))
        out_shape.append(jax.ShapeDtypeStruct((stack[0], m * n_heads, HEAD_W), F32))
    elif f32_layout == "heads":
        out_specs.append(pl.BlockSpec((tm * n_heads, HEAD_W), lambda i: (i, 0)))
        out_shape.append(jax.ShapeDtypeStruct((m * n_heads, HEAD_W), F32))
    if out_bf16:
        out_specs.append(pl.BlockSpec((tm, width), lambda i: (i, 0)))
        out_shape.append(jax.ShapeDtypeStruct((m, width), BF16))
    if transposed_tk:
        assert tm % transposed_tk == 0 and seq % tm == 0
        blocks_per_seq = seq // tm
        sub = tm // transposed_tk
        out_specs.append(pl.BlockSpec(
            (None, n_heads, sub, HEAD_W, transposed_tk),
            lambda i: (i // blocks_per_seq, 0, i % blocks_per_seq, 0, 0)))
        out_shape.append(jax.ShapeDtypeStruct(
            (m // seq, n_heads, seq // transposed_tk, HEAD_W, transposed_tk), BF16))
    assert off % width == 0
    in_specs = [pl.BlockSpec((tm, d), lambda i: (i, 0)),
                pl.BlockSpec((None, d, width), lambda i: (layer, 0, off // width)),
                tab_spec, tab_spec]
    operands = [h, w_in, cos, sin]
    aliased = stack is not None and stack[1] is not None
    aliases = {}
    if aliased:
        assert f32_layout == "heads"
        in_specs.append(pl.BlockSpec(memory_space=pl.ANY))
        operands.append(stack[1])
        aliases = {len(operands) - 1: 0}
    return pl.pallas_call(
        functools.partial(_heads_kernel, rope=rope, scale=scale, f32_layout=f32_layout,
                          out_bf16=out_bf16, transposed_tk=transposed_tk, aliased=aliased,
                          zero_fill=layer if (stack is not None and not aliased) else None),
        grid=(m // tm,),
        in_specs=in_specs,
        out_specs=out_specs,
        out_shape=out_shape,
        input_output_aliases=aliases,
        compiler_params=_params(("parallel",)),
        name="proj_heads",
    )(*operands)


def _cbranch_kernel(h_ref, wb_ref, wc_ref, wx_ref, cb_ref, cc_ref):
    h = h_ref[...]
    cb_ref[...] = _dot(h, wb_ref[...].astype(BF16)).astype(cb_ref.dtype)
    c_c = _dot(h, wc_ref[...].astype(BF16))
    c_x = _dot(h, wx_ref[...].astype(BF16))
    cc_ref[...] = c_c * c_x


def _proj_cbranch(h, w_in, layer, off_b, off_c, off_x, width, tm):
    m, d = h.shape
    tn = _tile(width, 512)
    return pl.pallas_call(
        _cbranch_kernel,
        grid=(m // tm, width // tn),
        in_specs=[pl.BlockSpec((tm, d), lambda i, j: (i, 0)),
                  _w_spec(d, tn, layer, off_b),
                  _w_spec(d, tn, layer, off_c),
                  _w_spec(d, tn, layer, off_x)],
        out_specs=[pl.BlockSpec((tm, tn), lambda i, j: (i, j)),
                   pl.BlockSpec((tm, tn), lambda i, j: (i, j))],
        out_shape=[jax.ShapeDtypeStruct((m, width), BF16),
                   jax.ShapeDtypeStruct((m, width), F32)],
        compiler_params=_params(("parallel", "arbitrary")),
        name="proj_cbranch",
    )(h, w_in, w_in, w_in)


def _dwconv_fill(u_ref, prev_ref, buf, hist_rows, tt):
    ti = pl.program_id(1)

    @pl.when(ti == 0)
    def _():
        buf[0:hist_rows, :] = prev_ref[...]

    @pl.when(ti > 0)
    def _():
        buf[0:hist_rows, :] = buf[tt:tt + hist_rows, :]

    buf[hist_rows:hist_rows + tt, :] = u_ref[...]


def _dwconv_rows(buf, w_ref, taps, hist_rows, r0, rows, c0, cols):
    base = hist_rows - (taps - 1)
    acc = None
    for k in range(taps):
        term = buf[r0 + base + k:r0 + base + k + rows, c0:c0 + cols] * w_ref[k:k + 1, c0:c0 + cols]
        acc = term if acc is None else acc + term
    return acc


def _conv_ln_silu_kernel(u_ref, prev_ref, w_ref, b_ref, g_ref, beta_ref, o_ref, buf, shifted,
                         ybuf, *, taps, hist_rows, tt, row_chunk, col_chunk):
    _dwconv_fill(u_ref, prev_ref, buf, hist_rows, tt)
    c = u_ref.shape[-1]
    base = hist_rows - (taps - 1)
    span = shifted.shape[1]
    for res in range(1, V7X_SUBLANES):
        shifted[res - 1] = buf[res:res + span, :]
    for r0 in range(0, tt, row_chunk):
        for c0 in range(0, c, col_chunk):
            acc = None
            for k in range(taps):
                res, tile_off = (base + k) % V7X_SUBLANES, (base + k) // V7X_SUBLANES * V7X_SUBLANES
                rows = slice(r0 + tile_off, r0 + tile_off + row_chunk)
                cols = slice(c0, c0 + col_chunk)
                window = buf[rows, cols] if res == 0 else shifted[res - 1, rows, cols]
                term = window * w_ref[k:k + 1, cols]
                acc = term if acc is None else acc + term
            ybuf[r0:r0 + row_chunk, c0:c0 + col_chunk] = acc
    y = ybuf[...] + b_ref[...]
    mu = jnp.mean(y, axis=-1, keepdims=True)
    yc = y - mu
    var = jnp.mean(yc * yc, axis=-1, keepdims=True)
    z = yc * lax.rsqrt(var + LN_EPS) * g_ref[...] + beta_ref[...]
    o_ref[...] = (z * jax.nn.sigmoid(z)).astype(o_ref.dtype)


def _conv_ln_silu(u, prev, w, bias, ln_g, ln_b, tt):
    b, t, c = u.shape
    taps = w.shape[0]
    hist_rows = prev.shape[1]
    row_chunk = min(tt, 32)
    col_chunk = min(c, 512)
    vec = lambda a: a.reshape(1, c)
    return pl.pallas_call(
        functools.partial(_conv_ln_silu_kernel, taps=taps, hist_rows=hist_rows, tt=tt,
                          row_chunk=row_chunk, col_chunk=col_chunk),
        grid=(b, t // tt),
        in_specs=[pl.BlockSpec((None, tt, c), lambda bi, ti: (bi, ti, 0)),
                  pl.BlockSpec((None, hist_rows, c), lambda bi, ti: (bi, 0, 0)),
                  pl.BlockSpec((taps, c), lambda bi, ti: (0, 0)),
                  pl.BlockSpec((1, c), lambda bi, ti: (0, 0)),
                  pl.BlockSpec((1, c), lambda bi, ti: (0, 0)),
                  pl.BlockSpec((1, c), lambda bi, ti: (0, 0))],
        out_specs=pl.BlockSpec((None, tt, c), lambda bi, ti: (bi, ti, 0)),
        out_shape=jax.ShapeDtypeStruct((b, t, c), BF16),
        scratch_shapes=[pltpu.VMEM((hist_rows + tt, c), F32),
                        pltpu.VMEM((V7X_SUBLANES - 1, tt + hist_rows - V7X_SUBLANES, c), F32),
                        pltpu.VMEM((tt, c), F32)],
        compiler_params=_params(("parallel", "arbitrary")),
        name="conv_a",
    )(u, prev, w, vec(bias), vec(ln_g), vec(ln_b))


def _conv_mul_kernel(u_ref, prev_ref, w_ref, cb_ref, o_ref, buf, *, taps, hist_rows, tt):
    _dwconv_fill(u_ref, prev_ref, buf, hist_rows, tt)
    c = u_ref.shape[-1]
    conv = _dwconv_rows(buf, w_ref, taps, hist_rows, 0, tt, 0, c)
    o_ref[...] = (cb_ref[...].astype(F32) * conv).astype(o_ref.dtype)


def _conv_mul(u, prev, w, cb, tt):
    b, t, c = u.shape
    taps = w.shape[0]
    hist_rows = prev.shape[1]
    return pl.pallas_call(
        functools.partial(_conv_mul_kernel, taps=taps, hist_rows=hist_rows, tt=tt),
        grid=(b, t // tt),
        in_specs=[pl.BlockSpec((None, tt, c), lambda bi, ti: (bi, ti, 0)),
                  pl.BlockSpec((None, hist_rows, c), lambda bi, ti: (bi, 0, 0)),
                  pl.BlockSpec((taps, c), lambda bi, ti: (0, 0)),
                  pl.BlockSpec((None, tt, c), lambda bi, ti: (bi, ti, 0))],
        out_specs=pl.BlockSpec((None, tt, c), lambda bi, ti: (bi, ti, 0)),
        out_shape=jax.ShapeDtypeStruct((b, t, c), BF16),
        scratch_shapes=[pltpu.VMEM((hist_rows + tt, c), F32)],
        compiler_params=_params(("parallel", "arbitrary")),
        name="conv_c",
    )(u, prev, w, cb)


def _lambda(lq1_ref, lk1_ref, lq2_ref, lk2_ref, lam_init):
    s1 = jnp.sum(lq1_ref[...] * lk1_ref[...], axis=-1, keepdims=True)
    s2 = jnp.sum(lq2_ref[...] * lk2_ref[...], axis=-1, keepdims=True)
    return jnp.exp(s1) - jnp.exp(s2) + lam_init


def _subln(o, g, lam_init):
    return _rms(o, g, SUBLN_EPS) * (1.0 - lam_init)


def _attn_prompt_kernel(q_ref, k_ref, vt_ref, lq1_ref, lk1_ref, lq2_ref, lk2_ref, gs_ref,
                        o_ref, s_ref, m_ref, l_ref, acc_ref, *, tb, lam_init, heads):
    qi = pl.program_id(2)
    lane = lax.broadcasted_iota(jnp.int32, (tb, HEAD_W), 1)
    q_maps = []
    for hh in range(heads):
        q = q_ref[:, hh * HEAD_W:(hh + 1) * HEAD_W]
        zero = jnp.zeros_like(q)
        q_maps.append((jnp.where(lane < HEAD_DIM, q, zero), jnp.where(lane >= HEAD_DIM, q, zero)))
    m_ref[...] = jnp.full(m_ref.shape, NEG_INF, F32)
    l_ref[...] = jnp.zeros(l_ref.shape, F32)
    acc_ref[...] = jnp.zeros(acc_ref.shape, F32)

    def scores(kj, slot):
        rows = pl.ds(pl.multiple_of(kj * tb, tb), tb)
        for hh in range(heads):
            k = k_ref[rows, hh * HEAD_W:(hh + 1) * HEAD_W]
            for mp in range(2):
                s_ref[slot, hh, mp] = _dot_nt(k, q_maps[hh][mp])

    def consume(kj, slot, masked):
        for hh in range(heads):
            vt = vt_ref[hh, kj]
            for mp in range(2):
                s = s_ref[slot, hh, mp]
                if masked:
                    key = lax.broadcasted_iota(jnp.int32, s.shape, 0)
                    qry = lax.broadcasted_iota(jnp.int32, s.shape, 1)
                    s = jnp.where(key <= qry, s, NEG_INF)
                m_prev = m_ref[hh, mp]
                m_new = jnp.maximum(m_prev, jnp.max(s, axis=0, keepdims=True))
                alpha = jnp.exp2(m_prev - m_new)
                p = jnp.exp2(s - m_new)
                l_ref[hh, mp] = alpha * l_ref[hh, mp] + jnp.sum(p, axis=0, keepdims=True)
                acc_ref[hh, mp] = alpha * acc_ref[hh, mp] + _dot(vt, p.astype(BF16))
                m_ref[hh, mp] = m_new

    scores(0, 0)

    def block_pair(t, carry):
        scores(2 * t + 1, 1)
        consume(2 * t, 0, False)
        scores(2 * t + 2, 0)
        consume(2 * t + 1, 1, False)
        return carry

    lax.fori_loop(0, qi // 2, block_pair, 0)

    @pl.when(qi % 2 == 0)
    def _():
        consume(qi, 0, True)

    @pl.when(qi % 2 == 1)
    def _():
        scores(qi, 1)
        consume(qi - 1, 0, False)
        consume(qi, 1, True)

    lam = _lambda(lq1_ref, lk1_ref, lq2_ref, lk2_ref, lam_init)
    for hh in range(heads):
        diff = acc_ref[hh, 0] / l_ref[hh, 0] - lam * (acc_ref[hh, 1] / l_ref[hh, 1])
        ms = jnp.mean(diff * diff, axis=0, keepdims=True)
        normed = diff * lax.rsqrt(ms + SUBLN_EPS) * gs_ref[...] * (1.0 - lam_init)
        o_ref[:, hh * HEAD_W:(hh + 1) * HEAD_W] = normed.T.astype(o_ref.dtype)


def _attn_prompt(qb, kb, vt, lam_params, g_sub, lam_init, batch, seq, tb):
    m, w = qb.shape
    n_heads = w // HEAD_W
    heads = _tile_count(n_heads, PROMPT_ATTN_HEADS_PER_STEP)
    hw = heads * HEAD_W
    nq = seq // tb
    small = lambda n: pl.BlockSpec((1, n), lambda b, h, qi: (0, 0))
    return pl.pallas_call(
        functools.partial(_attn_prompt_kernel, tb=tb, lam_init=lam_init, heads=heads),
        grid=(batch, n_heads // heads, nq),
        in_specs=[pl.BlockSpec((tb, hw), lambda b, h, qi: (b * nq + qi, h)),
                  pl.BlockSpec((seq, hw), lambda b, h, qi: (b, h)),
                  pl.BlockSpec((None, heads, nq, HEAD_W, tb), lambda b, h, qi: (b, h, 0, 0, 0)),
                  small(HEAD_DIM), small(HEAD_DIM), small(HEAD_DIM), small(HEAD_DIM),
                  pl.BlockSpec((HEAD_W, 1), lambda b, h, qi: (0, 0))],
        out_specs=pl.BlockSpec((tb, hw), lambda b, h, qi: (b * nq + qi, h)),
        out_shape=jax.ShapeDtypeStruct((m, w), BF16),
        scratch_shapes=[pltpu.VMEM((2, heads, 2, tb, tb), F32),
                        pltpu.VMEM((heads, 2, 1, tb), F32),
                        pltpu.VMEM((heads, 2, 1, tb), F32),
                        pltpu.VMEM((heads, 2, HEAD_W, tb), F32)],
        compiler_params=_params(("parallel", "parallel", "arbitrary")),
        name="attn_prompt",
    )(qb, kb, vt, *lam_params, g_sub.reshape(HEAD_W, 1))


def _attn_sample_kernel(pt_ref, q_ref, kn_ref, vn_ref, *refs, t_new, n_heads, lam_init, n_pages,
                        seqs):
    page_refs = refs[:2 * seqs * n_pages]
    (lq1_ref, lk1_ref, lq2_ref, lk2_ref, gs_ref, o_ref,
     qbd_ref, bias_ref, m_ref, l_ref, acc_ref) = refs[2 * seqs * n_pages:]
    del pt_ref
    p = pl.program_id(1)
    rows_per_head = 2 * t_new

    @pl.when(p == 0)
    def _():
        lane = lax.broadcasted_iota(jnp.int32, (t_new, HEAD_W), 1)
        for r in range(seqs):
            q = q_ref[r]
            for h in range(n_heads):
                qh = q[:, h * HEAD_W:(h + 1) * HEAD_W]
                r0 = h * rows_per_head
                qbd_ref[r, r0:r0 + t_new, :] = jnp.where(lane < HEAD_DIM, qh, 0.0)
                qbd_ref[r, r0 + t_new:r0 + rows_per_head, :] = jnp.where(lane >= HEAD_DIM, qh, 0.0)
        m_ref[...] = jnp.full(m_ref.shape, NEG_INF, F32)
        l_ref[...] = jnp.zeros(l_ref.shape, F32)
        acc_ref[...] = jnp.zeros(acc_ref.shape, F32)

        row = lax.broadcasted_iota(jnp.int32, bias_ref.shape, 0)
        col = lax.broadcasted_iota(jnp.int32, bias_ref.shape, 1)
        bias_ref[...] = jnp.where(col % n_heads == row // rows_per_head, 0.0, NEG_INF)

    def update(r, scores, values):
        m_prev = m_ref[r]
        m_new = m_prev
        for s in scores:
            m_new = jnp.maximum(m_new, jnp.max(s, axis=-1, keepdims=True))
        alpha = jnp.exp2(m_prev - m_new)
        l_new = alpha * l_ref[r]
        acc = alpha * acc_ref[r]
        for s, v in zip(scores, values):
            pr = jnp.exp2(s - m_new)
            l_new = l_new + jnp.sum(pr, axis=-1, keepdims=True)
            acc = acc + _dot(pr.astype(BF16), v.astype(BF16))
        l_ref[r] = l_new
        acc_ref[r] = acc
        m_ref[r] = m_new

    page_rows = PAGE_SIZE * n_heads
    bias = bias_ref[...]
    for r in range(seqs):
        qbd = qbd_ref[r].astype(BF16)
        k_refs = page_refs[r * n_pages:(r + 1) * n_pages]
        v_refs = page_refs[(seqs + r) * n_pages:(seqs + r + 1) * n_pages]
        scores = [_dot_nt(qbd, k_ref[...].reshape(page_rows, HEAD_W).astype(BF16)) + bias
                  for k_ref in k_refs]
        values = [v_ref[...].reshape(page_rows, HEAD_W) for v_ref in v_refs]
        update(r, scores, values)

    @pl.when(p == pl.num_programs(1) - 1)
    def _():
        lam = _lambda(lq1_ref, lk1_ref, lq2_ref, lk2_ref, lam_init)
        for r in range(seqs):
            s = _dot_nt(qbd_ref[r].astype(BF16), kn_ref[r].astype(BF16))
            row = lax.broadcasted_iota(jnp.int32, s.shape, 0)
            col = lax.broadcasted_iota(jnp.int32, s.shape, 1)
            valid = jnp.logical_and(col % n_heads == row // rows_per_head,
                                    col // n_heads <= row % t_new)
            update(r, [jnp.where(valid, s, NEG_INF)], [vn_ref[r]])
            o = acc_ref[r] / l_ref[r]
            for h in range(n_heads):
                r0 = h * rows_per_head
                diff = o[r0:r0 + t_new, :] - lam * o[r0 + t_new:r0 + rows_per_head, :]
                o_ref[r, :, h * HEAD_W:(h + 1) * HEAD_W] = _subln(
                    diff, gs_ref[...], lam_init).astype(o_ref.dtype)


def _attn_sample(q, k_new, v_new, cache_k, cache_v, page_table, layer, lam_params, g_sub,
                 lam_init):
    b, t_new, w = q.shape
    n_heads = w // HEAD_W
    total_pages = page_table.shape[1]
    n_pages = _tile_count(total_pages, SAMPLE_PAGES_PER_STEP)
    seqs = _tile_count(b, SAMPLE_SEQS_PER_STEP)
    assert t_new % V7X_SUBLANES == 0
    rows = 2 * n_heads * t_new

    q_spec = pl.BlockSpec((seqs, t_new, w), lambda bi, p, pt: (bi, 0, 0))
    new_spec = pl.BlockSpec((seqs, t_new * n_heads, HEAD_W), lambda bi, p, pt: (bi, 0, 0))
    page_shape = (None, None, PAGE_SIZE, n_heads, HEAD_W)
    page_specs = [
        pl.BlockSpec(page_shape, lambda bi, p, pt, r=r, c=c: (
            layer, pt[bi * seqs + r, n_pages * p + c], 0, 0, 0))
        for r in range(seqs) for c in range(n_pages)]
    small = lambda n: pl.BlockSpec((1, n), lambda bi, p, pt: (0, 0))
    grid_spec = pltpu.PrefetchScalarGridSpec(
        num_scalar_prefetch=1,
        grid=(b // seqs, total_pages // n_pages),
        in_specs=[q_spec, new_spec, new_spec] + page_specs + page_specs
                 + [small(HEAD_DIM), small(HEAD_DIM), small(HEAD_DIM), small(HEAD_DIM),
                    small(HEAD_W)],
        out_specs=pl.BlockSpec((seqs, t_new, w), lambda bi, p, pt: (bi, 0, 0)),
        scratch_shapes=[pltpu.VMEM((seqs, rows, HEAD_W), F32),
                        pltpu.VMEM((rows, PAGE_SIZE * n_heads), F32),
                        pltpu.VMEM((seqs, rows, 1), F32),
                        pltpu.VMEM((seqs, rows, 1), F32),
                        pltpu.VMEM((seqs, rows, HEAD_W), F32)],
    )
    n_page_ops = seqs * n_pages
    return pl.pallas_call(
        functools.partial(_attn_sample_kernel, t_new=t_new, n_heads=n_heads, lam_init=lam_init,
                          n_pages=n_pages, seqs=seqs),
        grid_spec=grid_spec,
        out_shape=jax.ShapeDtypeStruct((b, t_new, w), BF16),
        compiler_params=_params(("parallel", "arbitrary")),
        name="attn_sample",
    )(page_table, q, k_new, v_new, *([cache_k] * n_page_ops), *([cache_v] * n_page_ops),
      *lam_params, g_sub.reshape(1, HEAD_W))


def _merge_kernel(ta_ref, tb_ref, tc_ref, h_ref, wa_ref, wb_ref, wc_ref,
                  wga_ref, wgb_ref, wgc_ref, o_ref):
    h = h_ref[...]

    def branch(t_ref, w_ref, wg_ref):
        y = _dot(t_ref[...], w_ref[...].astype(BF16))
        g = _dot(h, wg_ref[...].astype(BF16))
        return jax.nn.sigmoid(g) * y

    merged = (branch(ta_ref, wa_ref, wga_ref) + branch(tb_ref, wb_ref, wgb_ref)
              + branch(tc_ref, wc_ref, wgc_ref))
    o_ref[...] = merged.astype(o_ref.dtype)


def _merge(ta, tb, tc, h, w_a_out, w_b_out, w_c_out, w_in, layer, gate_offs, tm):
    m, d = h.shape
    tn = _tile(d, 256)
    act = lambda a: pl.BlockSpec((tm, a.shape[1]), lambda i, j: (i, 0))
    wout = lambda a: pl.BlockSpec((None, a.shape[1], tn), lambda i, j: (layer, 0, j))
    return pl.pallas_call(
        _merge_kernel,
        grid=(m // tm, d // tn),
        in_specs=[act(ta), act(tb), act(tc), act(h),
                  wout(w_a_out), wout(w_b_out), wout(w_c_out),
                  _w_spec(d, tn, layer, gate_offs[0]),
                  _w_spec(d, tn, layer, gate_offs[1]),
                  _w_spec(d, tn, layer, gate_offs[2])],
        out_specs=pl.BlockSpec((tm, tn), lambda i, j: (i, j)),
        out_shape=jax.ShapeDtypeStruct((m, d), BF16),
        compiler_params=_params(("parallel", "arbitrary")),
        name="merge",
    )(ta, tb, tc, h, w_a_out, w_b_out, w_c_out, w_in, w_in, w_in)


def _residual_kernel(x_ref, lhs_ref, w_ref, g_ref, xo_ref, *h_refs, col_chunk, final):
    k = pl.program_id(1)

    @pl.when(k == 0)
    def _():
        xo_ref[...] = x_ref[...]

    lhs = lhs_ref[...]
    for c0 in range(0, xo_ref.shape[1], col_chunk):
        xo_ref[:, c0:c0 + col_chunk] += _dot(lhs, w_ref[:, c0:c0 + col_chunk].astype(BF16))

    @pl.when(k == pl.num_programs(1) - 1)
    def _():
        y = _rms(xo_ref[...], g_ref[...], RMS_EPS)
        if final:
            xo_ref[...] = y
        else:
            h_refs[0][...] = y.astype(h_refs[0].dtype)


def _residual_matmul_norm(x, lhs, w, layer, g_next, tm, final, name):
    m, d = x.shape
    kdim = lhs.shape[1]
    tk = _tile(kdim, 512)
    row_spec = pl.BlockSpec((tm, d), lambda i, k: (i, 0))
    out_specs = [row_spec] + ([] if final else [row_spec])
    out_shape = ([jax.ShapeDtypeStruct((m, d), F32)]
                 + ([] if final else [jax.ShapeDtypeStruct((m, d), BF16)]))
    outs = pl.pallas_call(
        functools.partial(_residual_kernel, col_chunk=_tile(d, 512), final=final),
        grid=(m // tm, kdim // tk),
        in_specs=[row_spec,
                  pl.BlockSpec((tm, tk), lambda i, k: (i, k)),
                  pl.BlockSpec((None, tk, d), lambda i, k: (layer, k, 0)),
                  pl.BlockSpec((1, d), lambda i, k: (0, 0))],
        out_specs=out_specs,
        out_shape=out_shape,
        compiler_params=_params(("parallel", "arbitrary")),
        name=name,
    )(x, lhs, w, g_next.reshape(1, d))
    return (outs[0], None) if final else tuple(outs)


def _ffn_up_kernel(h_ref, wg_ref, wv_ref, wc_ref, prev_ref, act_ref, gt_ref, buf_ref, hist_ref,
                   *, stream, seq_len, blocks_per_seq, sub):
    i = pl.program_id(0)
    j = pl.program_id(1)
    h = h_ref[...]
    tm = h.shape[0]
    fc = act_ref.shape[1]
    hist = V7X_SUBLANES
    nseq = tm // seq_len

    if stream:
        @pl.when(i % blocks_per_seq == 0)
        def _():
            buf_ref[:, 0:hist, :] = jnp.zeros((1, hist, fc), F32)

        @pl.when(i % blocks_per_seq != 0)
        def _():
            buf_ref[:, 0:hist, :] = hist_ref[j]
    else:
        buf_ref[:, 0:hist, :] = prev_ref[...]

    for c0 in range(0, fc, sub):
        cols = slice(c0, c0 + sub)
        gate = _dot(h, wg_ref[:, cols].astype(BF16))
        val = _dot(h, wv_ref[:, cols].astype(BF16))
        w = wc_ref[:, cols]
        if stream:
            hist_ref[j, :, :, cols] = gate[tm - hist:tm, :].reshape(1, hist, sub)
            gt_ref[:, cols] = gate[tm - hist:tm, :]
            prev_rows = buf_ref[0, 0:hist, cols]
            row = lax.broadcasted_iota(jnp.int32, (hist, sub), 0)
            shifted = []
            for shift in (2, 1):
                rolled = pltpu.roll(gate, shift, 0)
                top = jnp.where(row < shift, pltpu.roll(prev_rows, shift, 0), rolled[0:hist, :])
                shifted.append((top, rolled))

            def gated(rows):
                g2 = shifted[0][0] if rows is None else shifted[0][1][rows, :]
                g1 = shifted[1][0] if rows is None else shifted[1][1][rows, :]
                sl = slice(0, hist) if rows is None else rows
                conv = g2 * w[0:1, :] + g1 * w[1:2, :] + gate[sl, :] * w[2:3, :]
                return (conv * jax.nn.sigmoid(conv) * val[sl, :]).astype(act_ref.dtype)

            act_ref[0:hist, cols] = gated(None)
            act_ref[hist:tm, cols] = gated(slice(hist, tm))
        else:
            gt_ref[:, cols] = gate
            buf_ref[:, hist:hist + seq_len, cols] = gate.reshape(nseq, seq_len, sub)
            conv = (buf_ref[:, hist - 2:hist - 2 + seq_len, cols] * w[0:1, :].reshape(1, 1, sub)
                    + buf_ref[:, hist - 1:hist - 1 + seq_len, cols] * w[1:2, :].reshape(1, 1, sub)
                    + buf_ref[:, hist:hist + seq_len, cols] * w[2:3, :].reshape(1, 1, sub))
            conv = conv.reshape(tm, sub)
            act_ref[:, cols] = (conv * jax.nn.sigmoid(conv) * val).astype(act_ref.dtype)


def _ffn_up(h2, w_ffn_in, conv_w, layer, prev, seq_len, tm):
    m, d = h2.shape
    dff = conv_w.shape[-1]
    fc = _tile(dff, 512)
    nj = dff // fc
    stream = prev is None
    if stream:
        assert seq_len % tm == 0
        blocks_per_seq = seq_len // tm
        rows_in_block = tm
        prev = jnp.zeros((1, V7X_SUBLANES, dff), F32)
        prev_spec = pl.BlockSpec((1, V7X_SUBLANES, fc), lambda i, j: (0, 0, j))
        gt_shape = jax.ShapeDtypeStruct((m // tm, V7X_SUBLANES, dff), F32)
        gt_spec = pl.BlockSpec((None, V7X_SUBLANES, fc), lambda i, j: (i, 0, j))
        nseq = 1
    else:
        assert tm % seq_len == 0 and m == tm
        blocks_per_seq = 1
        rows_in_block = seq_len
        nseq = tm // seq_len
        prev_spec = pl.BlockSpec((nseq, V7X_SUBLANES, fc), lambda i, j: (0, 0, j))
        gt_shape = jax.ShapeDtypeStruct((m, dff), F32)
        gt_spec = pl.BlockSpec((tm, fc), lambda i, j: (i, j))
    return pl.pallas_call(
        functools.partial(_ffn_up_kernel, stream=stream, seq_len=rows_in_block,
                          blocks_per_seq=blocks_per_seq,
                          sub=fc // 2 if fc % (4 * V7X_LANES) == 0 else fc),
        grid=(m // tm, nj),
        in_specs=[pl.BlockSpec((tm, d), lambda i, j: (i, 0)),
                  pl.BlockSpec((None, d, fc), lambda i, j: (layer, 0, j)),
                  pl.BlockSpec((None, d, fc), lambda i, j: (layer, 0, nj + j)),
                  pl.BlockSpec((None, conv_w.shape[1], fc), lambda i, j: (layer, 0, j)),
                  prev_spec],
        out_specs=[pl.BlockSpec((tm, fc), lambda i, j: (i, j)), gt_spec],
        out_shape=[jax.ShapeDtypeStruct((m, dff), BF16), gt_shape],
        scratch_shapes=[pltpu.VMEM((nseq, V7X_SUBLANES + rows_in_block, fc), F32),
                        pltpu.VMEM((nj, 1, V7X_SUBLANES, fc), F32)],
        compiler_params=_params(("arbitrary", "arbitrary")),
        name="ffn_up",
    )(h2, w_ffn_in, w_ffn_in, conv_w, prev)


def _rope_tables(pos):
    half = HEAD_DIM // 2
    inv_freq = 1.0 / (ROPE_THETA ** (jnp.arange(half, dtype=F32) * 2.0 / HEAD_DIM))
    ang = pos.astype(F32)[:, None] * inv_freq[None, :]
    cos, sin = jnp.cos(ang), jnp.sin(ang)
    reps = V7X_LANES // HEAD_DIM
    cos_t = jnp.concatenate([cos, cos] * reps, axis=1)
    sin_t = jnp.concatenate([-sin, sin] * reps, axis=1)
    return cos_t, sin_t


def _pad_history(state, rows):
    b, k1, c = state.shape
    return jnp.concatenate([jnp.zeros((b, rows - k1, c), state.dtype), state], axis=1)


def _layer(x, h, layer, lam_init, final, group, weights):
    (w_in, conv_a_w, conv_a_b, ln_a_g, ln_a_b, w_a_out, lam_q1, lam_k1, lam_q2, lam_k2,
     subln_g, w_b_out, conv_c_w, w_c_out, w_o, norm_ffn, w_ffn_in, conv_ffn_w, w_ffn_down,
     g_next) = weights
    batch, seq, tm = group["batch"], group["seq"], group["tm"]
    m, d = x.shape
    d_a = conv_a_w.shape[-1]
    d_c = conv_c_w.shape[-1]
    d_ff = conv_ffn_w.shape[-1]
    qk_w = w_b_out.shape[1]
    taps_a, taps_c = conv_a_w.shape[1], conv_c_w.shape[1]
    offs = {}
    off = 0
    for name, width in (("a_val", d_a), ("a_gate", d_a), ("q", qk_w), ("k", qk_w), ("v", qk_w),
                        ("c_b", d_c), ("c_c", d_c), ("c_x", d_c),
                        ("g_a", d), ("g_b", d), ("g_c", d)):
        offs[name] = off
        off += width
    assert off == w_in.shape[-1]
    is_sample = group["cache"] is not None
    scale = HEAD_DIM ** -0.5 * LOG2_E
    lam_params = tuple(a[layer].reshape(1, HEAD_DIM) for a in (lam_q1, lam_k1, lam_q2, lam_k2))

    u_a = _proj_glu(h, w_in, layer, offs["a_val"], offs["a_gate"], d_a, tm)
    u_a3 = u_a.reshape(batch, seq, d_a)
    hist_a = -(-(taps_a - 1) // V7X_SUBLANES) * V7X_SUBLANES
    if is_sample:
        prev_a = _pad_history(group["state_a"][layer], hist_a)
        new_a = jnp.concatenate([group["state_a"][layer], u_a3], axis=1)[:, -(taps_a - 1):]
    else:
        prev_a = jnp.zeros((batch, hist_a, d_a), F32)
        new_a = u_a3[:, seq - (taps_a - 1):]
    t_a = _conv_ln_silu(u_a3, prev_a, conv_a_w[layer], conv_a_b[layer], ln_a_g[layer],
                        ln_a_b[layer], group["tt"]).reshape(m, d_a)

    cos, sin = group["rope"]
    n_heads = qk_w // HEAD_W
    proj = functools.partial(_proj_heads, h, w_in, layer, width=qk_w, cos=cos, sin=sin, tm=tm)
    if is_sample:
        (q_f,) = proj(off=offs["q"], rope=True, scale=scale, f32_layout="rows")
        (k_f,) = proj(off=offs["k"], rope=True, f32_layout="heads")
        (v_f,) = proj(off=offs["v"], rope=False, f32_layout="heads")
        cache_k, cache_v, page_table = group["cache"]
        t_b = _attn_sample(q_f.reshape(batch, seq, qk_w),
                           k_f.reshape(batch, seq * n_heads, HEAD_W),
                           v_f.reshape(batch, seq * n_heads, HEAD_W),
                           cache_k, cache_v, page_table, layer,
                           lam_params, subln_g[layer], lam_init).reshape(m, qk_w)
    else:
        tb = _tile(seq, PROMPT_ATTN_BLOCK)
        (q_b,) = proj(off=offs["q"], rope=True, scale=scale, out_bf16=True)
        depth = w_in.shape[0]
        k_prev, v_prev = group.get("kv_stack", (None, None))
        k_f, k_b = proj(off=offs["k"], rope=True, f32_layout="heads", out_bf16=True,
                        stack=(depth, k_prev))
        v_f, v_t = proj(off=offs["v"], rope=False, f32_layout="heads", transposed_tk=tb, seq=seq,
                        stack=(depth, v_prev))
        group["kv_stack"] = (k_f, v_f)
        t_b = _attn_prompt(q_b, k_b, v_t, lam_params, subln_g[layer], lam_init, batch, seq, tb)

    c_b, cc = _proj_cbranch(h, w_in, layer, offs["c_b"], offs["c_c"], offs["c_x"], d_c, tm)
    cc3 = cc.reshape(batch, seq, d_c)
    if is_sample:
        prev_c = _pad_history(group["state_c"][layer], V7X_SUBLANES)
        new_c = jnp.concatenate([group["state_c"][layer], cc3], axis=1)[:, -(taps_c - 1):]
    else:
        prev_c = jnp.zeros((batch, V7X_SUBLANES, d_c), F32)
        new_c = cc3[:, seq - (taps_c - 1):]
    t_c = _conv_mul(cc3, prev_c, conv_c_w[layer], c_b.reshape(batch, seq, d_c),
                    group["tt"]).reshape(m, d_c)

    merged = _merge(t_a, t_b, t_c, h, w_a_out, w_b_out, w_c_out, w_in, layer,
                    (offs["g_a"], offs["g_b"], offs["g_c"]), tm)
    x_mid, h2 = _residual_matmul_norm(x, merged, w_o, layer, norm_ffn[layer], tm, False,
                                      "wo_residual_norm")

    taps_f = conv_ffn_w.shape[1]
    if is_sample:
        prev_f = _pad_history(group["state_f"][layer], V7X_SUBLANES)
        act, g_up = _ffn_up(h2, w_ffn_in, conv_ffn_w, layer, prev_f, seq, tm)
        new_f = jnp.concatenate([group["state_f"][layer], g_up.reshape(batch, seq, d_ff)],
                                axis=1)[:, -(taps_f - 1):]
    else:
        tm_up = _tile(seq, FFN_UP_ROWS)
        act, g_tail = _ffn_up(h2, w_ffn_in, conv_ffn_w, layer, None, seq, tm_up)
        blocks_per_seq = seq // tm_up
        g_tail = g_tail.reshape(batch, blocks_per_seq, V7X_SUBLANES, d_ff)
        new_f = g_tail[:, blocks_per_seq - 1, V7X_SUBLANES - (taps_f - 1):]
    x_new, h_next = _residual_matmul_norm(x_mid, act, w_ffn_down, layer, g_next, tm, final,
                                          "ffn_down_residual_norm")
    lead = () if is_sample else (w_in.shape[0],)
    k_out = k_f.reshape(*lead, batch, seq, n_heads, HEAD_W)
    v_out = v_f.reshape(*lead, batch, seq, n_heads, HEAD_W)
    return x_new, h_next, k_out, v_out, new_a, new_c, new_f


def kernel(x_prompt, x_sample, cache_k, cache_v, state_conv_a, state_conv_c, state_conv_ffn,
           page_table, norm_mix, w_in, conv_a_w, conv_a_b, ln_a_g, ln_a_b, w_a_out, lam_q1,
           lam_k1, lam_q2, lam_k2, subln_g, w_b_out, conv_c_w, w_c_out, w_o, norm_ffn, w_ffn_in,
           conv_ffn_w, w_ffn_down, norm_final):
    bp, t_p, d = x_prompt.shape
    bs, t_s, _ = x_sample.shape
    depth = w_in.shape[0]
    past = page_table.shape[1] * PAGE_SIZE

    tm_p = _tile(t_p, 1024)
    cos_p, sin_p = _rope_tables(jnp.arange(t_p, dtype=jnp.int32))
    cos_s, sin_s = _rope_tables(past + jnp.arange(t_s, dtype=jnp.int32))
    groups = [
        dict(batch=bp, seq=t_p, tm=tm_p, tt=_tile(t_p, 256),
             rope=(cos_p, sin_p), cache=None),
        dict(batch=bs, seq=t_s, tm=bs * t_s, tt=t_s,
             rope=(jnp.tile(cos_s, (bs, 1)), jnp.tile(sin_s, (bs, 1))),
             cache=(cache_k, cache_v, page_table),
             state_a=state_conv_a, state_c=state_conv_c, state_f=state_conv_ffn),
    ]
    xs = [x_prompt.reshape(bp * t_p, d), x_sample.reshape(bs * t_s, d)]
    hs = [_rmsnorm_bf16(x, norm_mix[0]) for x in xs]
    collected = [[], []]
    for layer in range(depth):
        lam_init = 0.8 - 0.6 * math.exp(-0.3 * layer)
        final = layer == depth - 1
        g_next = norm_final if final else norm_mix[layer + 1]
        weights = (w_in, conv_a_w, conv_a_b, ln_a_g, ln_a_b, w_a_out, lam_q1, lam_k1, lam_q2,
                   lam_k2, subln_g, w_b_out, conv_c_w, w_c_out, w_o, norm_ffn, w_ffn_in,
                   conv_ffn_w, w_ffn_down, g_next)
        for gi, group in enumerate(groups):
            x_new, h_next, k_out, v_out, new_a, new_c, new_f = _layer(
                xs[gi], hs[gi], layer, lam_init, final, group, weights)
            xs[gi], hs[gi] = x_new, h_next
            collected[gi].append((k_out, v_out, new_a, new_c, new_f))
    y_prompt = xs[0].reshape(bp, t_p, d)
    y_sample = xs[1].reshape(bs, t_s, d)
    stack = lambda gi, idx: jnp.stack([c[idx] for c in collected[gi]])
    return (y_prompt, y_sample,
            collected[0][-1][0], collected[0][-1][1], stack(0, 2), stack(0, 3), stack(0, 4),
            stack(1, 0), stack(1, 1), stack(1, 2), stack(1, 3), stack(1, 4))
```
